```python
import jax
import jax.numpy as jnp
from jax import lax
import numpy as np

D_MODEL = 2048
BATCH = 16
SEQ = 2048
DEPTH = 1
DEC_BATCH = 32
DEC_SEQ = 64
PAST_LEN = 4096

CHUNK = 64
Q_BLOCK = 128
MLA_HEADS = 8
MLA_NOPE = 128
MLA_ROPE = 64
MLA_VDIM = 128
MLA_QK = MLA_NOPE + MLA_ROPE
MLA_WIDTH = MLA_HEADS * MLA_VDIM
Q_LORA = D_MODEL // 4
KV_LORA = D_MODEL // 4
RWKV_HEADS = 16
RWKV_HEAD = 64
RWKV_WIDTH = RWKV_HEADS * RWKV_HEAD
D_MIX = MLA_WIDTH + RWKV_WIDTH
DECAY_LORA = 64
AAA_LORA = 64
GATE_LORA = 160
MLA_PROJ = Q_LORA + KV_LORA + MLA_ROPE
RWKV_PROJ = 3 * RWKV_WIDTH + DECAY_LORA + AAA_LORA + GATE_LORA
IN_PROJ = MLA_PROJ + RWKV_PROJ
N_GROUPS = 8
EXPERTS_PER_GROUP = 8
N_EXPERTS = N_GROUPS * EXPERTS_PER_GROUP
TOP_K_IN_GROUP = 2
D_EXPERT = D_MODEL // 4
MOE_BLOCK = 128
ROPE_THETA = 10000.0
LN_EPS = 1e-5
RMS_EPS = 1e-6
GN_EPS = 64e-5
DECAY_SCALE = 0.606531
SOFTMAX_SCALE = MLA_QK ** -0.5
DEEPNORM_ALPHA = (2.0 * DEPTH) ** 0.25
DEEPNORM_BETA = (8.0 * DEPTH) ** -0.25

kernel_name = 'hymba_mla_rwkv7_hmoe_deepnorm_stream_step'


def layer_norm(x, g, b):
    xf = x.astype(jnp.float32)
    mu = jnp.mean(xf, axis=-1, keepdims=True)
    var = jnp.mean(jnp.square(xf - mu), axis=-1, keepdims=True)
    return ((xf - mu) * lax.rsqrt(var + LN_EPS)).astype(x.dtype) * g + b


def rms_norm(x, g):
    xf = x.astype(jnp.float32)
    return (xf * lax.rsqrt(jnp.mean(jnp.square(xf), axis=-1, keepdims=True) + RMS_EPS)).astype(x.dtype) * g


def rope_cos_sin(pos):
    inv_freq = ROPE_THETA ** (-jnp.arange(0, MLA_ROPE, 2, dtype=jnp.float32) / MLA_ROPE)
    ang = pos.astype(jnp.float32)[:, None] * inv_freq[None, :]
    return jnp.cos(ang), jnp.sin(ang)


def apply_rope(x, cos, sin):
    x1, x2 = jnp.split(x.astype(jnp.float32), 2, axis=-1)
    return jnp.concatenate([x1 * cos - x2 * sin, x2 * cos + x1 * sin], axis=-1).astype(x.dtype)


def mla_attention(q_nope, q_rope, kv_all, kr_all, w_uk, w_uv, n_past):
    S = q_nope.shape[1]
    outs = []
    for q0 in range(0, S, Q_BLOCK):
        q1 = min(q0 + Q_BLOCK, S)
        kend = n_past + q1
        q_pos = jnp.arange(n_past + q0, n_past + q1)
        k_pos = jnp.arange(kend)
        kv = kv_all[:, :kend]
        q_lat = jnp.einsum('bqhn,chn->bqhc', q_nope[:, q0:q1], w_uk)
        s = (jnp.einsum('bqhc,bkc->bhqk', q_lat, kv, preferred_element_type=jnp.float32)
             + jnp.einsum('bqhr,bkr->bhqk', q_rope[:, q0:q1], kr_all[:, :kend],
                          preferred_element_type=jnp.float32)) * SOFTMAX_SCALE
        visible = (k_pos[None, :] // CHUNK) <= (q_pos[:, None] // CHUNK)
        s = jnp.where(visible[None, None], s, -jnp.inf)
        p = jax.nn.softmax(s, axis=-1).astype(kv.dtype)
        o_lat = jnp.einsum('bhqk,bkc->bqhc', p, kv)
        outs.append(jnp.einsum('bqhc,chv->bqhv', o_lat, w_uv))
    return jnp.concatenate(outs, axis=1)


def rwkv7_time_mix(rw, shift_prev, wkv_prev, P):
    B, S, _ = rw.shape
    f32 = jnp.float32
    c = RWKV_WIDTH
    prev = jnp.concatenate([shift_prev.astype(rw.dtype), rw[:, :-1]], axis=1)
    xm = rw + (prev - rw) * P['rwkv_mu']
    r, k, v, wd, ad, gd = jnp.split(
        xm, [c, 2 * c, 3 * c, 3 * c + DECAY_LORA, 3 * c + DECAY_LORA + AAA_LORA], axis=-1)
    decay = jnp.exp(-DECAY_SCALE * jax.nn.sigmoid(
        (P['rwkv_w0'] + jnp.tanh(wd) @ P['rwkv_w2']).astype(f32)))
    a = jax.nn.sigmoid((P['rwkv_a0'] + ad @ P['rwkv_a2']).astype(f32))
    g = jax.nn.sigmoid(gd) @ P['rwkv_g2']

    def heads(t):
        return t.astype(f32).reshape(B, S, RWKV_HEADS, RWKV_HEAD)

    kk = heads(k * P['rwkv_k_k'])
    kk = kk / jnp.maximum(jnp.sqrt(jnp.sum(jnp.square(kk), axis=-1, keepdims=True)), 1e-12)
    k_mod = k.astype(f32) * (1.0 + (a - 1.0) * P['rwkv_k_a'].astype(f32))
    rh, wh, kh, vh, ah = heads(r), heads(decay), heads(k_mod), heads(v), heads(a)

    def step(state, inp):
        r_t, w_t, k_t, v_t, kk_t, a_t = inp
        sk = jnp.einsum('bhvk,bhk->bhv', state, kk_t)
        state = (state * w_t[:, :, None, :] - sk[..., None] * (kk_t * a_t)[:, :, None, :]
                 + v_t[..., None] * k_t[:, :, None, :])
        return state, jnp.einsum('bhvk,bhk->bhv', state, r_t)

    xs = tuple(jnp.swapaxes(t, 0, 1) for t in (rh, wh, kh, vh, kk, ah))
    wkv_new, y = lax.scan(step, wkv_prev.astype(f32), xs)
    y = jnp.swapaxes(y, 0, 1)
    mu = jnp.mean(y, axis=-1, keepdims=True)
    var = jnp.mean(jnp.square(y - mu), axis=-1, keepdims=True)
    y = ((y - mu) * lax.rsqrt(var + GN_EPS)).reshape(B, S, c) * P['rwkv_lnx_g'] + P['rwkv_lnx_b']
    bonus = jnp.sum(rh * kh * P['rwkv_r_k'].astype(f32), axis=-1, keepdims=True) * vh
    y = (y + bonus.reshape(B, S, c)) * g
    return y.astype(rw.dtype), rw[:, -1:], wkv_new.astype(rw.dtype)


def hier_moe(u, P):
    B, S, D = u.shape
    f32 = jnp.float32
    T = B * S
    A = T * TOP_K_IN_GROUP
    t = u.reshape(T, D)
    g_prob = jax.nn.softmax((t @ P['router_group_w']).astype(f32) + P['router_group_b'].astype(f32), axis=-1)
    p_grp, grp = lax.top_k(g_prob, 1)
    e_logits = ((t @ P['router_expert_w']).astype(f32) + P['router_expert_b'].astype(f32)
                ).reshape(T, N_GROUPS, EXPERTS_PER_GROUP)
    e_in_grp = jnp.take_along_axis(e_logits, grp[:, :, None], axis=1)[:, 0]
    top_v, top_i = lax.top_k(e_in_grp, TOP_K_IN_GROUP)
    gate = p_grp * jax.nn.softmax(top_v, axis=-1)
    eid = (grp * EXPERTS_PER_GROUP + top_i).reshape(A)
    wts = gate.reshape(A)
    tok = jnp.arange(A, dtype=jnp.int32) // TOP_K_IN_GROUP
    order = jnp.argsort(eid)
    eid_s, tok_s, wts_s = eid[order], tok[order], wts[order]
    counts = jnp.bincount(eid, length=N_EXPERTS)
    starts = jnp.cumsum(counts) - counts
    padded = (counts + MOE_BLOCK - 1) // MOE_BLOCK * MOE_BLOCK
    pad_ends = jnp.cumsum(padded)
    pad_starts = pad_ends - padded
    dest = pad_starts[eid_s] + jnp.arange(A, dtype=jnp.int32) - starts[eid_s]
    n_blocks = -(-(A + N_EXPERTS * (MOE_BLOCK - 1)) // MOE_BLOCK)
    R = n_blocks * MOE_BLOCK
    row_tok = jnp.zeros((R,), jnp.int32).at[dest].set(tok_s)
    row_w = jnp.zeros((R,), f32).at[dest].set(wts_s)
    block_expert = jnp.minimum(
        jnp.searchsorted(pad_ends, jnp.arange(n_blocks, dtype=jnp.int32) * MOE_BLOCK, side='right'),
        N_EXPERTS - 1)
    xb = t[row_tok].reshape(n_blocks, MOE_BLOCK, D)

    def expert_block(args):
        xe, e = args
        h = jax.nn.silu(xe @ P['expert_w_gate'][e]) * (xe @ P['expert_w_up'][e])
        return h @ P['expert_w_down'][e]

    yb = lax.map(expert_block, (xb, block_expert)).reshape(R, D)
    out = jnp.zeros((T, D), u.dtype).at[row_tok].add(row_w[:, None].astype(u.dtype) * yb)
    return out.reshape(B, S, D)


def trunk_layer(x, c, kv_past, kr_past, shift_prev, wkv_prev, P):
    B, S, _ = x.shape
    n_past = 0 if kv_past is None else kv_past.shape[1]
    mod = jax.nn.silu(c) @ P['w_ada'] + P['b_ada']
    sh1, sc1, gt1, sh2, sc2, gt2 = jnp.split(mod[:, None, :], 6, axis=-1)
    u = x * (1 + sc1) + sh1
    proj = u @ P['w_in']
    q_d, kv_d, kr_d, rw = jnp.split(proj, [Q_LORA, Q_LORA + KV_LORA, MLA_PROJ], axis=-1)
    cos, sin = rope_cos_sin(jnp.arange(n_past, n_past + S))
    q = (rms_norm(q_d, P['q_norm_g']) @ P['w_uq']).reshape(B, S, MLA_HEADS, MLA_QK)
    q_nope = q[..., :MLA_NOPE]
    q_rope = apply_rope(q[..., MLA_NOPE:], cos[:, None, :], sin[:, None, :])
    kv_new = rms_norm(kv_d, P['kv_norm_g'])
    kr_new = apply_rope(kr_d, cos, sin)
    if kv_past is None:
        kv_all, kr_all = kv_new, kr_new
    else:
        kv_all = jnp.concatenate([kv_past.astype(kv_new.dtype), kv_new], axis=1)
        kr_all = jnp.concatenate([kr_past.astype(kr_new.dtype), kr_new], axis=1)
    mla_out = mla_attention(q_nope, q_rope, kv_all, kr_all, P['w_uk'], P['w_uv'], n_past)
    mla_out = mla_out.reshape(B, S, MLA_WIDTH)
    rwkv_out, shift_new, wkv_new = rwkv7_time_mix(rw, shift_prev, wkv_prev, P)
    mix = jnp.concatenate([mla_out, rwkv_out], axis=-1) @ P['w_out']
    x1 = layer_norm(DEEPNORM_ALPHA * x + gt1 * mix, P['ln1_g'], P['ln1_b'])
    u2 = x1 * (1 + sc2) + sh2
    x2 = layer_norm(DEEPNORM_ALPHA * x1 + gt2 * hier_moe(u2, P), P['ln2_g'], P['ln2_b'])
    return x2, kv_new, kr_new, shift_new, wkv_new


def setup_inputs(seed: int = 0) -> dict:
    key = jax.random.key(seed)
    ks = list(jax.random.split(key, 48))
    L = DEPTH

    def nrm(i, shape, scale):
        return scale * jax.random.normal(ks[i], shape, jnp.float32)

    return {
        'x_prompt': nrm(0, (BATCH, SEQ, D_MODEL), 1.0),
        'x_sample': nrm(1, (DEC_BATCH, DEC_SEQ, D_MODEL), 1.0),
        'c_prompt': nrm(2, (BATCH, D_MODEL), 1.0),
        'c_sample': nrm(3, (DEC_BATCH, D_MODEL), 1.0),
        'cache_kv_latent': nrm(4, (L, DEC_BATCH, PAST_LEN, KV_LORA), 1.0),
        'cache_k_rope': nrm(5, (L, DEC_BATCH, PAST_LEN, MLA_ROPE), 1.0),
        'state_shift': nrm(6, (L, DEC_BATCH, 1, RWKV_PROJ), 1.0),
        'state_wkv': nrm(7, (L, DEC_BATCH, RWKV_HEADS, RWKV_HEAD, RWKV_HEAD), 0.3),
        'w_ada': nrm(8, (L, D_MODEL, 6 * D_MODEL), 0.5 * D_MODEL ** -0.5),
        'b_ada': nrm(9, (L, 6 * D_MODEL), 0.1),
        'w_in': nrm(10, (L, D_MODEL, IN_PROJ), D_MODEL ** -0.5),
        'q_norm_g': 1.0 + nrm(11, (L, Q_LORA), 0.02),
        'w_uq': nrm(12, (L, Q_LORA, MLA_HEADS * MLA_QK), Q_LORA ** -0.5),
        'kv_norm_g': 1.0 + nrm(13, (L, KV_LORA), 0.02),
        'w_uk': nrm(14, (L, KV_LORA, MLA_HEADS, MLA_NOPE), KV_LORA ** -0.5),
        'w_uv': nrm(15, (L, KV_LORA, MLA_HEADS, MLA_VDIM), KV_LORA ** -0.5),
        'rwkv_mu': jax.random.uniform(ks[16], (L, RWKV_PROJ), jnp.float32),
        'rwkv_w0': 0.5 + nrm(17, (L, RWKV_WIDTH), 1.0),
        'rwkv_w2': nrm(18, (L, DECAY_LORA, RWKV_WIDTH), 0.1 * DECAY_LORA ** -0.5),
        'rwkv_a0': nrm(19, (L, RWKV_WIDTH), 0.5),
        'rwkv_a2': nrm(20, (L, AAA_LORA, RWKV_WIDTH), 0.5 * AAA_LORA ** -0.5),
        'rwkv_g2': nrm(21, (L, GATE_LORA, RWKV_WIDTH), GATE_LORA ** -0.5),
        'rwkv_k_k': 0.85 + nrm(22, (L, RWKV_WIDTH), 0.05),
        'rwkv_k_a': 1.0 + nrm(23, (L, RWKV_WIDTH), 0.05),
        'rwkv_r_k': nrm(24, (L, RWKV_HEADS, RWKV_HEAD), 0.1),
        'rwkv_lnx_g': 1.0 + nrm(25, (L, RWKV_WIDTH), 0.02),
        'rwkv_lnx_b': nrm(26, (L, RWKV_WIDTH), 0.02),
        'w_out': nrm(27, (L, D_MIX, D_MODEL), DEEPNORM_BETA * D_MIX ** -0.5),
        'ln1_g': 1.0 + nrm(28, (L, D_MODEL), 0.02),
        'ln1_b': nrm(29, (L, D_MODEL), 0.02),
        'router_group_w': nrm(30, (L, D_MODEL, N_GROUPS), D_MODEL ** -0.5),
        'router_group_b': nrm(31, (L, N_GROUPS), 0.01),
        'router_expert_w': nrm(32, (L, D_MODEL, N_EXPERTS), D_MODEL ** -0.5),
        'router_expert_b': nrm(33, (L, N_EXPERTS), 0.01),
        'expert_w_gate': nrm(34, (L, N_EXPERTS, D_MODEL, D_EXPERT), D_MODEL ** -0.5),
        'expert_w_up': nrm(35, (L, N_EXPERTS, D_MODEL, D_EXPERT), D_MODEL ** -0.5),
        'expert_w_down': nrm(36, (L, N_EXPERTS, D_EXPERT, D_MODEL), DEEPNORM_BETA * D_EXPERT ** -0.5),
        'ln2_g': 1.0 + nrm(37, (L, D_MODEL), 0.02),
        'ln2_b': nrm(38, (L, D_MODEL), 0.02),
    }


def reference(x_prompt, x_sample, c_prompt, c_sample, cache_kv_latent, cache_k_rope, state_shift,
              state_wkv, w_ada, b_ada, w_in, q_norm_g, w_uq, kv_norm_g, w_uk, w_uv, rwkv_mu, rwkv_w0,
              rwkv_w2, rwkv_a0, rwkv_a2, rwkv_g2, rwkv_k_k, rwkv_k_a, rwkv_r_k, rwkv_lnx_g, rwkv_lnx_b,
              w_out, ln1_g, ln1_b, router_group_w, router_group_b, router_expert_w, router_expert_b,
              expert_w_gate, expert_w_up, expert_w_down, ln2_g, ln2_b):
    h_p, h_s = x_prompt, x_sample
    kv_p, kr_p, sh_p, wkv_p = [], [], [], []
    kv_s, kr_s, sh_s, wkv_s = [], [], [], []
    for l in range(DEPTH):
        P = dict(w_ada=w_ada[l], b_ada=b_ada[l], w_in=w_in[l], q_norm_g=q_norm_g[l], w_uq=w_uq[l],
                 kv_norm_g=kv_norm_g[l], w_uk=w_uk[l], w_uv=w_uv[l], rwkv_mu=rwkv_mu[l],
                 rwkv_w0=rwkv_w0[l], rwkv_w2=rwkv_w2[l], rwkv_a0=rwkv_a0[l], rwkv_a2=rwkv_a2[l],
                 rwkv_g2=rwkv_g2[l], rwkv_k_k=rwkv_k_k[l], rwkv_k_a=rwkv_k_a[l], rwkv_r_k=rwkv_r_k[l],
                 rwkv_lnx_g=rwkv_lnx_g[l], rwkv_lnx_b=rwkv_lnx_b[l], w_out=w_out[l], ln1_g=ln1_g[l],
                 ln1_b=ln1_b[l], router_group_w=router_group_w[l], router_group_b=router_group_b[l],
                 router_expert_w=router_expert_w[l], router_expert_b=router_expert_b[l],
                 expert_w_gate=expert_w_gate[l], expert_w_up=expert_w_up[l],
                 expert_w_down=expert_w_down[l], ln2_g=ln2_g[l], ln2_b=ln2_b[l])
        zero_shift = jnp.zeros((h_p.shape[0], 1, RWKV_PROJ), h_p.dtype)
        zero_wkv = jnp.zeros((h_p.shape[0], RWKV_HEADS, RWKV_HEAD, RWKV_HEAD), jnp.float32)
        h_p, a_kv, a_kr, a_sh, a_wkv = trunk_layer(h_p, c_prompt, None, None, zero_shift, zero_wkv, P)
        kv_p.append(a_kv)
        kr_p.append(a_kr)
        sh_p.append(a_sh)
        wkv_p.append(a_wkv)
        h_s, b_kv, b_kr, b_sh, b_wkv = trunk_layer(h_s, c_sample, cache_kv_latent[l], cache_k_rope[l],
                                                   state_shift[l], state_wkv[l], P)
        kv_s.append(b_kv)
        kr_s.append(b_kr)
        sh_s.append(b_sh)
        wkv_s.append(b_wkv)
    return (h_p, h_s, jnp.stack(kv_p), jnp.stack(kr_p), jnp.stack(sh_p), jnp.stack(wkv_p),
            jnp.stack(kv_s), jnp.stack(kr_s), jnp.stack(sh_s), jnp.stack(wkv_s))
```

```python
import functools

import jax
import jax.numpy as jnp
from jax import lax
from jax.experimental import pallas as pl
from jax.experimental.pallas import tpu as pltpu

F32 = jnp.float32
BF16 = jnp.bfloat16

LANES = 128
SUBLANES = 8

D_MODEL = 2048
CHUNK = 64
MLA_HEADS = 8
MLA_NOPE = 128
MLA_ROPE = 64
MLA_VDIM = 128
MLA_QK = MLA_NOPE + MLA_ROPE
MLA_WIDTH = MLA_HEADS * MLA_VDIM
Q_LORA = 512
KV_LORA = 512
RWKV_HEADS = 16
RWKV_HEAD = 64
RWKV_WIDTH = RWKV_HEADS * RWKV_HEAD
DECAY_LORA = 64
AAA_LORA = 64
GATE_LORA = 160
RWKV_PROJ = 3 * RWKV_WIDTH + DECAY_LORA + AAA_LORA + GATE_LORA
N_GROUPS = 8
EXPERTS_PER_GROUP = 8
N_EXPERTS = N_GROUPS * EXPERTS_PER_GROUP
TOP_K = 2
D_EXPERT = 512
MOE_BLOCK = 128
ROPE_THETA = 10000.0
LN_EPS = 1e-5
RMS_EPS = 1e-6
GN_EPS = 64e-5
DECAY_SCALE = 0.606531
SOFTMAX_SCALE = MLA_QK ** -0.5
DEEPNORM_ALPHA = 2.0 ** 0.25

MLA_SLOT = Q_LORA + KV_LORA + LANES
RW_LORA_SLOT = DECAY_LORA + AAA_LORA
RW_GATE_SLOT = 2 * LANES
RW_SLOT = 3 * RWKV_WIDTH + RW_LORA_SLOT + RW_GATE_SLOT
ROW_TILE = 512
ATT_TQ = 256
ATT_TK_PAST = 512
RWKV_TT = 128
PAIR = 2 * RWKV_HEAD
VMEM_LIMIT = 56 * 1024 * 1024


def _dot(a, b):
    return jnp.dot(a.astype(BF16), b.astype(BF16), preferred_element_type=F32)


def _dot_nt(a, b):
    return lax.dot_general(a.astype(BF16), b.astype(BF16), (((1,), (1,)), ((), ())),
                           preferred_element_type=F32)


def _dot_tn(a, b):
    return lax.dot_general(a.astype(BF16), b.astype(BF16), (((0,), (0,)), ((), ())),
                           preferred_element_type=F32)


def _split3(x):
    h = x.astype(BF16)
    r1 = x - h.astype(F32)
    m = r1.astype(BF16)
    l = (r1 - m.astype(F32)).astype(BF16)
    return h, m, l


def _iota(shape, dim):
    return lax.broadcasted_iota(jnp.int32, shape, dim)


def _params(sem):
    return pltpu.CompilerParams(dimension_semantics=sem, vmem_limit_bytes=VMEM_LIMIT)


def _ada_kernel(c_ref, w_ref, b_ref, o_ref):
    c = c_ref[...]
    s = c * jax.nn.sigmoid(c)
    o_ref[...] = _dot(s, w_ref[...]) + b_ref[...]


def _ada(c_all, w_ada, b_ada):
    nb, d = c_all.shape
    n = w_ada.shape[1]
    tn = 1536
    return pl.pallas_call(
        _ada_kernel,
        grid=(n // tn,),
        in_specs=[pl.BlockSpec((nb, d), lambda j: (0, 0)),
                  pl.BlockSpec((d, tn), lambda j: (0, j)),
                  pl.BlockSpec((1, tn), lambda j: (0, j))],
        out_specs=pl.BlockSpec((nb, tn), lambda j: (0, j)),
        out_shape=jax.ShapeDtypeStruct((nb, n), F32),
        compiler_params=_params(("arbitrary",)),
    )(c_all, w_ada, b_ada.reshape(1, n))


def _inproj_kernel(x_ref, mod_ref, w_ref, om_ref, or_ref, u_ref):
    n = pl.program_id(2)
    gb, ts, d = x_ref.shape

    @pl.when(n == 0)
    def _():
        u = x_ref[...] * (1.0 + mod_ref[:, 1:2, :]) + mod_ref[:, 0:1, :]
        u_ref[...] = u.reshape(gb * ts, d).astype(BF16)

    res = jnp.dot(u_ref[...], w_ref[...], preferred_element_type=F32)

    @pl.when(n == 0)
    def _():
        om_ref[...] = res.reshape(om_ref.shape)

    @pl.when(n > 0)
    def _():
        or_ref[...] = res.reshape(or_ref.shape)


def _inproj(x, mod, w_packed, gb, ts):
    b, s, d = x.shape
    tn = MLA_SLOT
    nt = w_packed.shape[1] // tn
    return pl.pallas_call(
        _inproj_kernel,
        grid=(b // gb, s // ts, nt),
        in_specs=[pl.BlockSpec((gb, ts, d), lambda i, j, n: (i, j, 0)),
                  pl.BlockSpec((gb, 6, d), lambda i, j, n: (i, 0, 0)),
                  pl.BlockSpec((d, tn), lambda i, j, n: (0, n))],
        out_specs=[pl.BlockSpec((gb, ts, tn), lambda i, j, n: (i, j, 0)),
                   pl.BlockSpec((gb, ts, tn), lambda i, j, n: (i, j, jnp.maximum(n - 1, 0)))],
        out_shape=[jax.ShapeDtypeStruct((b, s, MLA_SLOT), F32),
                   jax.ShapeDtypeStruct((b, s, RW_SLOT), F32)],
        scratch_shapes=[pltpu.VMEM((gb * ts, d), BF16)],
        compiler_params=_params(("arbitrary", "arbitrary", "arbitrary")),
    )(x, mod, w_packed)


def _rms(x, g):
    return x * lax.rsqrt(jnp.mean(jnp.square(x), axis=-1, keepdims=True) + RMS_EPS) * g


def _rope_slot(slot, cos4, sin4):
    return slot * cos4 + pltpu.roll(slot, MLA_ROPE // 2, axis=1) * sin4


def _attn_kernel(*refs, tq, n_past, tk_past):
    if n_past:
        (pm_ref, cos_ref, sin_ref, gq_ref, gkv_ref, wuq_ref, wuk_ref, wuv_ref, ckv_ref, ckr_ref,
         mla_ref, kv_ref, kr_ref, qlat_ref, qrope_ref, m_ref, l_ref, acc_ref) = refs
    else:
        (pm_ref, cos_ref, sin_ref, gq_ref, gkv_ref, wuq_ref, wuk_ref, wuv_ref,
         mla_ref, kv_ref, kr_ref, qlat_ref, qrope_ref, m_ref, l_ref, acc_ref, kvs_ref, krs_ref) = refs
    j = pl.program_id(1)
    rows = MLA_HEADS * tq
    p = pm_ref[0]
    cos4 = cos_ref[...]
    sin4 = sin_ref[...]

    kv = _rms(p[:, Q_LORA:Q_LORA + KV_LORA], gkv_ref[...])
    kr = _rope_slot(p[:, Q_LORA + KV_LORA:MLA_SLOT], cos4, sin4)[:, :MLA_ROPE]
    kv_ref[0] = kv
    kr_ref[0] = kr
    kv_b = kv.astype(BF16)
    kr_b = kr.astype(BF16)

    q = _dot(_rms(p[:, :Q_LORA], gq_ref[...]), wuq_ref[...])
    for h in range(MLA_HEADS):
        qlat = _dot(q[:, h * MLA_NOPE:(h + 1) * MLA_NOPE], wuk_ref[h])
        rot = _rope_slot(q[:, MLA_WIDTH + h * LANES:MLA_WIDTH + (h + 1) * LANES], cos4, sin4)
        qlat_ref[h * tq:(h + 1) * tq, :] = (qlat * SOFTMAX_SCALE).astype(BF16)
        qrope_ref[h * tq:(h + 1) * tq, :] = (rot[:, :MLA_ROPE] * SOFTMAX_SCALE).astype(BF16)

    m_ref[...] = jnp.full(m_ref.shape, -jnp.inf, F32)
    l_ref[...] = jnp.zeros(l_ref.shape, F32)
    acc_ref[...] = jnp.zeros(acc_ref.shape, F32)

    def flash_step(kvb, krb, mask):
        s = _dot_nt(qlat_ref[...], kvb) + _dot_nt(qrope_ref[...], krb)
        if mask is not None:
            s = jnp.where(mask, s, -jnp.inf)
        m_prev = m_ref[...]
        m_new = jnp.maximum(m_prev, jnp.max(s, axis=-1, keepdims=True))
        alpha = jnp.exp(m_prev - m_new)
        pr = jnp.exp(s - m_new)
        l_ref[...] = alpha * l_ref[...] + jnp.sum(pr, axis=-1, keepdims=True)
        acc_ref[...] = alpha * acc_ref[...] + _dot(pr, kvb)
        m_ref[...] = m_new

    if n_past:
        def past_body(kb, carry):
            off = pl.multiple_of(kb * tk_past, tk_past)
            flash_step(ckv_ref[0, pl.ds(off, tk_past), :].astype(BF16),
                       ckr_ref[0, pl.ds(off, tk_past), :].astype(BF16), None)
            return carry
        lax.fori_loop(0, n_past // tk_past, past_body, 0)
        flash_step(kv_b, kr_b, None)
    else:
        off_j = pl.multiple_of(j * tq, tq)
        kvs_ref[pl.ds(off_j, tq), :] = kv_b
        krs_ref[pl.ds(off_j, tq), :] = kr_b

        def prev_body(kb, carry):
            off = pl.multiple_of(kb * tq, tq)
            flash_step(kvs_ref[pl.ds(off, tq), :], krs_ref[pl.ds(off, tq), :], None)
            return carry
        lax.fori_loop(0, j, prev_body, 0)
        q_chunk = (_iota((rows, tq), 0) & (tq - 1)) >> 6
        k_chunk = _iota((rows, tq), 1) >> 6
        flash_step(kv_b, kr_b, k_chunk <= q_chunk)

    o = acc_ref[...] / l_ref[...]
    for h in range(MLA_HEADS):
        mla_ref[0, :, h * MLA_VDIM:(h + 1) * MLA_VDIM] = _dot(o[h * tq:(h + 1) * tq, :], wuv_ref[h]).astype(BF16)


def _attn(pm, cos4, sin4, gq, gkv, wuq_p, wuk_t, wuv_t, cache_kv, cache_kr, tq):
    b, s, _ = pm.shape
    n_past = 0 if cache_kv is None else cache_kv.shape[1]
    rows = MLA_HEADS * tq
    const2 = lambda i, j: (0, 0)
    const3 = lambda i, j: (0, 0, 0)
    in_specs = [pl.BlockSpec((1, tq, MLA_SLOT), lambda i, j: (i, j, 0)),
                pl.BlockSpec((tq, LANES), lambda i, j: (j, 0)),
                pl.BlockSpec((tq, LANES), lambda i, j: (j, 0)),
                pl.BlockSpec((1, Q_LORA), const2),
                pl.BlockSpec((1, KV_LORA), const2),
                pl.BlockSpec(wuq_p.shape, const2),
                pl.BlockSpec(wuk_t.shape, const3),
                pl.BlockSpec(wuv_t.shape, const3)]
    args = [pm, cos4, sin4, gq, gkv, wuq_p, wuk_t, wuv_t]
    scratch = [pltpu.VMEM((rows, KV_LORA), BF16), pltpu.VMEM((rows, MLA_ROPE), BF16),
               pltpu.VMEM((rows, 1), F32), pltpu.VMEM((rows, 1), F32), pltpu.VMEM((rows, KV_LORA), F32)]
    if n_past:
        in_specs += [pl.BlockSpec((1, n_past, KV_LORA), lambda i, j: (i, 0, 0)),
                     pl.BlockSpec((1, n_past, MLA_ROPE), lambda i, j: (i, 0, 0))]
        args += [cache_kv, cache_kr]
    else:
        scratch += [pltpu.VMEM((s, KV_LORA), BF16), pltpu.VMEM((s, MLA_ROPE), BF16)]
    return pl.pallas_call(
        functools.partial(_attn_kernel, tq=tq, n_past=n_past, tk_past=min(ATT_TK_PAST, max(n_past, 1))),
        grid=(b, s // tq),
        in_specs=in_specs,
        out_specs=[pl.BlockSpec((1, tq, MLA_WIDTH), lambda i, j: (i, j, 0)),
                   pl.BlockSpec((1, tq, KV_LORA), lambda i, j: (i, j, 0)),
                   pl.BlockSpec((1, tq, MLA_ROPE), lambda i, j: (i, j, 0))],
        out_shape=[jax.ShapeDtypeStruct((b, s, MLA_WIDTH), BF16),
                   jax.ShapeDtypeStruct((b, s, KV_LORA), F32),
                   jax.ShapeDtypeStruct((b, s, MLA_ROPE), F32)],
        scratch_shapes=scratch,
        compiler_params=_params(("arbitrary", "arbitrary")),
    )(*args)


def _rwkv_kernel(pr_ref, shift_ref, zt0_ref, mu_ref, vec_ref, w2_ref, a2_ref, g2_ref,
                 out_ref, ztn_ref,
                 z_ref, carry_ref, y_ref, kt_ref, rt_ref, kh_ref, bh_ref, kb_ref, bb_ref, v_ref, wc_ref, *, tt):
    j = pl.program_id(1)
    nj = pl.num_programs(1)
    c = RWKV_WIDTH
    n_pairs = RWKV_HEADS // 2
    n_chunks = tt // CHUNK

    bd_mask = (_iota((PAIR, PAIR), 0) >> 6) == (_iota((PAIR, PAIR), 1) >> 6)

    @pl.when(j == 0)
    def _():
        carry_ref[...] = jnp.broadcast_to(shift_ref[0], carry_ref.shape)
        for pr in range(n_pairs):
            z_ref[pr] = jnp.zeros((PAIR, PAIR), F32)
            z_ref[pr, 0:RWKV_HEAD, 0:RWKV_HEAD] = zt0_ref[0, 2 * pr]
            z_ref[pr, RWKV_HEAD:PAIR, RWKV_HEAD:PAIR] = zt0_ref[0, 2 * pr + 1]

    rw = pr_ref[0]
    row = _iota((tt, 1), 0)
    prev = jnp.where(row == 0, carry_ref[0:1, :], pltpu.roll(rw, 1, axis=0))
    carry_ref[...] = jnp.broadcast_to(rw[tt - 1:tt, :], carry_ref.shape)
    xm = rw + (prev - rw) * mu_ref[...]

    w0 = vec_ref[0:1, :]
    a0 = vec_ref[1:2, :]
    k_k = vec_ref[2:3, :]
    k_a = vec_ref[3:4, :]
    r_k = vec_ref[4:5, :]
    lnx_g = vec_ref[5:6, :]
    lnx_b = vec_ref[6:7, :]

    r = xm[:, 0:c]
    k = xm[:, c:2 * c]
    v = xm[:, 2 * c:3 * c]
    lora = xm[:, 3 * c:3 * c + RW_LORA_SLOT]
    gd = xm[:, 3 * c + RW_LORA_SLOT:RW_SLOT]
    lw = -DECAY_SCALE * jax.nn.sigmoid(w0 + _dot(jnp.tanh(lora), w2_ref[...]))
    a = jax.nn.sigmoid(a0 + _dot(lora, a2_ref[...]))
    g = _dot(jax.nn.sigmoid(gd), g2_ref[...])

    ones_bd = jnp.where(bd_mask, 1.0, 0.0).astype(BF16)

    def head_sum(x):
        h, m, _ = _split3(x)
        cols = []
        for blk in range(c // LANES):
            sl = slice(blk * LANES, (blk + 1) * LANES)
            cols.append(jnp.dot(h[:, sl], ones_bd, preferred_element_type=F32)
                        + jnp.dot(m[:, sl], ones_bd, preferred_element_type=F32))
        return jnp.concatenate(cols, axis=1)

    kk = k * k_k
    kk = kk / jnp.maximum(jnp.sqrt(head_sum(kk * kk)), 1e-12)
    k_mod = k * (1.0 + (a - 1.0) * k_a)
    bonus = head_sum(r * k_mod * r_k) * v
    b_vec = kk * a

    tri = ((_iota((tt, tt), 0) >> 6) == (_iota((tt, tt), 1) >> 6)) & (_iota((tt, tt), 1) <= _iota((tt, tt), 0))
    tri_b = jnp.where(tri, 1.0, 0.0).astype(BF16)
    lw_h, lw_m, lw_l = _split3(lw)
    cum = (jnp.dot(tri_b, lw_h, preferred_element_type=F32) + jnp.dot(tri_b, lw_m, preferred_element_type=F32)
           + jnp.dot(tri_b, lw_l, preferred_element_type=F32))
    cum_end = jnp.concatenate(
        [jnp.broadcast_to(cum[(ci + 1) * CHUNK - 1:(ci + 1) * CHUNK, :], (CHUNK, c)) for ci in range(n_chunks)],
        axis=0)
    for ci in range(n_chunks):
        wc_ref[ci * SUBLANES:(ci + 1) * SUBLANES, :] = jnp.broadcast_to(
            jnp.exp(cum[(ci + 1) * CHUNK - 1:(ci + 1) * CHUNK, :]), (SUBLANES, c))
    e_inv = jnp.exp(-cum)
    e_end = jnp.exp(cum_end - cum)
    kt_ref[...] = (kk * jnp.exp(cum - lw)).astype(BF16)
    rt_ref[...] = (r * jnp.exp(cum)).astype(BF16)
    kh_ref[...] = (k_mod * e_inv).astype(BF16)
    bh_ref[...] = (b_vec * e_inv).astype(BF16)
    kb_ref[...] = (k_mod * e_end).astype(BF16)
    bb_ref[...] = (b_vec * e_end).astype(BF16)
    v_ref[...] = v.astype(BF16)

    t_i = _iota((CHUNK, PAIR), 0)
    j_i = _iota((CHUNK, PAIR), 1) & (RWKV_HEAD - 1)
    strict2 = j_i < t_i
    incl2 = j_i <= t_i
    blk16 = (j_i >> 4) == (t_i >> 4)
    eye2 = jnp.where(j_i == t_i, 1.0, 0.0)
    zero_b = jnp.zeros((PAIR, PAIR), BF16)

    def stack2(x):
        xb = x.astype(BF16)
        return jnp.where(bd_mask, jnp.concatenate([xb, xb], axis=0), zero_b)

    def pmm(x, y):
        return jnp.dot(x.astype(BF16), stack2(y), preferred_element_type=F32)

    for ci in range(n_chunks):
        rs = slice(ci * CHUNK, (ci + 1) * CHUNK)
        for pr in range(n_pairs):
            cs = slice(pr * PAIR, (pr + 1) * PAIR)
            kt = kt_ref[rs, cs]
            rt = rt_ref[rs, cs]
            vv = v_ref[rs, cs]
            a4 = _dot_nt(jnp.concatenate([kt, rt], axis=0),
                         jnp.concatenate([stack2(kh_ref[rs, cs]), stack2(bh_ref[rs, cs])], axis=0))
            m_mat = jnp.where(strict2, a4[:CHUNK, :PAIR], 0.0)
            n_mat = jnp.where(strict2, a4[:CHUNK, PAIR:], 0.0)
            p_mat = jnp.where(incl2, a4[CHUNK:, :PAIR], 0.0)
            q_mat = jnp.where(incl2, a4[CHUNK:, PAIR:], 0.0)
            n_d = jnp.where(blk16, n_mat, 0.0)
            n_o = n_mat - n_d
            n2 = pmm(n_d, n_d)
            n4 = pmm(n2, n2)
            n8 = pmm(n4, n4)
            t_d = pmm(pmm(pmm(eye2 - n_d, eye2 + n2), eye2 + n4), eye2 + n8)
            x1 = pmm(t_d, n_o)
            x2 = pmm(x1, x1)
            t_mat = pmm(pmm(eye2 - x1, eye2 + x2), t_d)
            mvpv = jnp.dot(jnp.concatenate([m_mat, p_mat], axis=0).astype(BF16), stack2(vv),
                           preferred_element_type=F32)
            mv = mvpv[:CHUNK]
            pv = mvpv[CHUNK:]
            tkmv = jnp.dot(t_mat.astype(BF16), jnp.concatenate([stack2(kt), stack2(mv)], axis=1),
                           preferred_element_type=F32)
            tk = tkmv[:, :PAIR]
            tmv = tkmv[:, PAIR:]
            z = z_ref[pr]
            zr = jnp.dot(jnp.concatenate([tk.astype(BF16), rt], axis=0), z.astype(BF16),
                         preferred_element_type=F32)
            sk = zr[:CHUNK] + tmv
            y_ref[rs, cs] = zr[CHUNK:] + pv - pmm(q_mat, sk)
            upd = _dot_tn(jnp.concatenate([kb_ref[rs, cs], bb_ref[rs, cs]], axis=0),
                          jnp.concatenate([vv, (-sk).astype(BF16)], axis=0))
            wc_col = jnp.transpose(wc_ref[ci * SUBLANES:(ci + 1) * SUBLANES, cs])[:, 0:1]
            z_ref[pr] = z * wc_col + jnp.where(bd_mask, upd, 0.0)

    y = y_ref[...]
    mean = head_sum(y) * (1.0 / RWKV_HEAD)
    dlt = y - mean
    var = head_sum(dlt * dlt) * (1.0 / RWKV_HEAD)
    yn = dlt * lax.rsqrt(var + GN_EPS) * lnx_g + lnx_b
    out_ref[0] = ((yn + bonus) * g).astype(BF16)

    @pl.when(j == nj - 1)
    def _():
        for pr in range(n_pairs):
            ztn_ref[0, 2 * pr] = z_ref[pr, 0:RWKV_HEAD, 0:RWKV_HEAD]
            ztn_ref[0, 2 * pr + 1] = z_ref[pr, RWKV_HEAD:PAIR, RWKV_HEAD:PAIR]


def _rwkv(pr, shift_p, zt0, mu_p, vecs, w2_p, a2_p, g2_p, tt):
    b, s, _ = pr.shape
    c = RWKV_WIDTH
    const2 = lambda i, j: (0, 0)
    full = lambda shape: pl.BlockSpec(shape, const2)
    return pl.pallas_call(
        functools.partial(_rwkv_kernel, tt=tt),
        grid=(b, s // tt),
        in_specs=[pl.BlockSpec((1, tt, RW_SLOT), lambda i, j: (i, j, 0)),
                  pl.BlockSpec((1, 1, RW_SLOT), lambda i, j: (i, 0, 0)),
                  pl.BlockSpec((1, RWKV_HEADS, RWKV_HEAD, RWKV_HEAD), lambda i, j: (i, 0, 0, 0)),
                  full((1, RW_SLOT)), full((SUBLANES, c)),
                  full(w2_p.shape), full(a2_p.shape), full(g2_p.shape)],
        out_specs=[pl.BlockSpec((1, tt, c), lambda i, j: (i, j, 0)),
                   pl.BlockSpec((1, RWKV_HEADS, RWKV_HEAD, RWKV_HEAD), lambda i, j: (i, 0, 0, 0))],
        out_shape=[jax.ShapeDtypeStruct((b, s, c), BF16),
                   jax.ShapeDtypeStruct((b, RWKV_HEADS, RWKV_HEAD, RWKV_HEAD), F32)],
        scratch_shapes=[pltpu.VMEM((RWKV_HEADS // 2, PAIR, PAIR), F32),
                        pltpu.VMEM((SUBLANES, RW_SLOT), F32),
                        pltpu.VMEM((tt, c), F32)]
                       + [pltpu.VMEM((tt, c), BF16) for _ in range(7)]
                       + [pltpu.VMEM((SUBLANES * (tt // CHUNK), c), F32)],
        compiler_params=_params(("arbitrary", "arbitrary")),
    )(pr, shift_p, zt0, mu_p, vecs, w2_p, a2_p, g2_p)


def _layer_norm(h, g, b):
    mu = jnp.mean(h, axis=-1, keepdims=True)
    d = h - mu
    var = jnp.mean(d * d, axis=-1, keepdims=True)
    return d * lax.rsqrt(var + LN_EPS) * g + b


def _outproj_kernel(mla_ref, rwk_ref, x_ref, mod_ref, wo1_ref, wo2_ref, ln_ref, wr_ref, rb_ref,
                    x1_ref, u2_ref, route_ref):
    gb, ts, d = x_ref.shape
    rows = gb * ts
    mix = (jnp.dot(mla_ref[...].reshape(rows, MLA_WIDTH), wo1_ref[...], preferred_element_type=F32)
           + jnp.dot(rwk_ref[...].reshape(rows, RWKV_WIDTH), wo2_ref[...], preferred_element_type=F32))
    h = DEEPNORM_ALPHA * x_ref[...] + mod_ref[:, 2:3, :] * mix.reshape(gb, ts, d)
    x1 = _layer_norm(h, ln_ref[0:1, :], ln_ref[1:2, :])
    u2 = (x1 * (1.0 + mod_ref[:, 4:5, :]) + mod_ref[:, 3:4, :]).reshape(rows, d)
    x1_ref[...] = x1.reshape(rows, d)
    u2_ref[...] = u2

    uh, um, ul = _split3(u2)
    lg = (jnp.dot(uh, wr_ref[0], preferred_element_type=F32) + jnp.dot(um, wr_ref[0], preferred_element_type=F32)
          + jnp.dot(uh, wr_ref[1], preferred_element_type=F32) + jnp.dot(ul, wr_ref[0], preferred_element_type=F32)
          + jnp.dot(um, wr_ref[1], preferred_element_type=F32) + jnp.dot(uh, wr_ref[2], preferred_element_type=F32)
          ) + rb_ref[...]
    lane = _iota((rows, LANES), 1).astype(F32)
    neg = -jnp.inf
    big = float(LANES)
    gl = jnp.where(lane < N_GROUPS, lg, neg)
    gmax = jnp.max(gl, axis=-1, keepdims=True)
    p_grp = 1.0 / jnp.sum(jnp.exp(gl - gmax), axis=-1, keepdims=True)
    grp = jnp.min(jnp.where(gl == gmax, lane, big), axis=-1, keepdims=True)
    lo = N_GROUPS + grp * EXPERTS_PER_GROUP
    el = jnp.where((lane >= lo) & (lane < lo + EXPERTS_PER_GROUP), lg, neg)
    v1 = jnp.max(el, axis=-1, keepdims=True)
    i1 = jnp.min(jnp.where(el == v1, lane, big), axis=-1, keepdims=True)
    el2 = jnp.where(lane == i1, neg, el)
    v2 = jnp.max(el2, axis=-1, keepdims=True)
    i2 = jnp.min(jnp.where(el2 == v2, lane, big), axis=-1, keepdims=True)
    e2 = jnp.exp(v2 - v1)
    den = 1.0 / (1.0 + e2)
    route_ref[...] = jnp.where(lane == 0, i1 - N_GROUPS,
                               jnp.where(lane == 1, i2 - N_GROUPS,
                                         jnp.where(lane == 2, p_grp * den,
                                                   jnp.where(lane == 3, p_grp * e2 * den, 0.0))))


def _outproj(mla, rwk, x, mod, wo1, wo2, ln, wr3, rb, gb, ts):
    b, s, d = x.shape
    rows = gb * ts
    nj = s // ts
    const2 = lambda i, j: (0, 0)
    flat = lambda i, j: (i * nj + j, 0)
    return pl.pallas_call(
        _outproj_kernel,
        grid=(b // gb, nj),
        in_specs=[pl.BlockSpec((gb, ts, MLA_WIDTH), lambda i, j: (i, j, 0)),
                  pl.BlockSpec((gb, ts, RWKV_WIDTH), lambda i, j: (i, j, 0)),
                  pl.BlockSpec((gb, ts, d), lambda i, j: (i, j, 0)),
                  pl.BlockSpec((gb, 6, d), lambda i, j: (i, 0, 0)),
                  pl.BlockSpec(wo1.shape, const2), pl.BlockSpec(wo2.shape, const2),
                  pl.BlockSpec(ln.shape, const2),
                  pl.BlockSpec(wr3.shape, lambda i, j: (0, 0, 0)),
                  pl.BlockSpec(rb.shape, const2)],
        out_specs=[pl.BlockSpec((rows, d), flat), pl.BlockSpec((rows, d), flat),
                   pl.BlockSpec((rows, LANES), flat)],
        out_shape=[jax.ShapeDtypeStruct((b * s, d), F32), jax.ShapeDtypeStruct((b * s, d), F32),
                   jax.ShapeDtypeStruct((b * s, LANES), F32)],
        compiler_params=_params(("arbitrary", "arbitrary")),
    )(mla, rwk, x, mod, wo1, wo2, ln, wr3, rb)


def _moe_kernel(bexp_ref, code_ref, nused_ref,
                u2_hbm, roww_ref, wg_ref, wu_ref, wd_ref,
                y_hbm,
                xbuf, ybuf, wgb, wub, wdb, gsem, ssem, *, n_tok):
    i = pl.program_id(0)
    nused = nused_ref[0]
    slot = i % 2

    def gather_copy(src_row, blk_slot, r):
        return pltpu.make_async_copy(u2_hbm.at[pl.ds(src_row, 1)], xbuf.at[blk_slot, pl.ds(r, 1)],
                                     gsem.at[blk_slot])

    def scatter_copy(dst_row, blk_slot, r):
        return pltpu.make_async_copy(ybuf.at[blk_slot, pl.ds(r, 1)], y_hbm.at[pl.ds(dst_row, 1)],
                                     ssem.at[blk_slot])

    def issue_gather(blk, blk_slot):
        def body(r, carry):
            gather_copy(code_ref[blk * MOE_BLOCK + r] >> 2, blk_slot, r).start()
            return carry
        lax.fori_loop(0, MOE_BLOCK, body, 0, unroll=8)

    def wait_gather(blk_slot):
        def body(r, carry):
            gather_copy(0, blk_slot, r).wait()
            return carry
        lax.fori_loop(0, MOE_BLOCK, body, 0, unroll=8)

    def scatter_rows(blk, blk_slot, start):
        def body(r, carry):
            code = code_ref[blk * MOE_BLOCK + r]

            @pl.when((code & 1) == 1)
            def _():
                cp = scatter_copy(((code >> 1) & 1) * n_tok + (code >> 2), blk_slot, r)
                if start:
                    cp.start()
                else:
                    cp.wait()
            return carry
        lax.fori_loop(0, MOE_BLOCK, body, 0, unroll=8)

    @pl.when(i < nused)
    def _():
        @pl.when(i == 0)
        def _():
            issue_gather(0, 0)

        @pl.when(i + 1 < nused)
        def _():
            issue_gather(i + 1, 1 - slot)

        wait_gather(slot)

        prev_e = bexp_ref[jnp.maximum(i - 1, 0)]

        @pl.when((i == 0) | (bexp_ref[i] != prev_e))
        def _():
            wgb[...] = wg_ref[0].astype(BF16)
            wub[...] = wu_ref[0].astype(BF16)
            wdb[...] = wd_ref[0].astype(BF16)

        xb = xbuf[slot].astype(BF16)
        hg = jnp.dot(xb, wgb[...], preferred_element_type=F32)
        hu = jnp.dot(xb, wub[...], preferred_element_type=F32)
        hh = (hg * jax.nn.sigmoid(hg) * hu).astype(BF16)
        yv = jnp.dot(hh, wdb[...], preferred_element_type=F32) * roww_ref[...]

        @pl.when(i >= 2)
        def _():
            scatter_rows(i - 2, slot, False)

        ybuf[slot] = yv
        scatter_rows(i, slot, True)

        @pl.when(i == nused - 1)
        def _():
            scatter_rows(i, slot, False)

            @pl.when(i >= 1)
            def _():
                scatter_rows(i - 1, 1 - slot, False)


def _moe(u2_all, code, row_w, block_expert, n_used, wg, wu, wd):
    n_tok, d = u2_all.shape
    n_blocks = block_expert.shape[0]
    de = wg.shape[2]

    def wmap(i, bexp, code_r, nused):
        return (bexp[jnp.minimum(i, nused[0] - 1)], 0, 0)

    def rmap(i, bexp, code_r, nused):
        return (jnp.minimum(i, nused[0] - 1), 0)

    grid_spec = pltpu.PrefetchScalarGridSpec(
        num_scalar_prefetch=3,
        grid=(n_blocks,),
        in_specs=[pl.BlockSpec(memory_space=pl.ANY),
                  pl.BlockSpec((MOE_BLOCK, 1), rmap),
                  pl.BlockSpec((1, d, de), wmap),
                  pl.BlockSpec((1, d, de), wmap),
                  pl.BlockSpec((1, de, d), wmap)],
        out_specs=pl.BlockSpec(memory_space=pl.ANY),
        scratch_shapes=[pltpu.VMEM((2, MOE_BLOCK, d), F32), pltpu.VMEM((2, MOE_BLOCK, d), F32),
                        pltpu.VMEM((d, de), BF16), pltpu.VMEM((d, de), BF16), pltpu.VMEM((de, d), BF16),
                        pltpu.SemaphoreType.DMA((2,)), pltpu.SemaphoreType.DMA((2,))])
    return pl.pallas_call(
        functools.partial(_moe_kernel, n_tok=n_tok),
        grid_spec=grid_spec,
        out_shape=jax.ShapeDtypeStruct((TOP_K * n_tok, d), F32),
        compiler_params=_params(("arbitrary",)),
    )(block_expert, code, n_used, u2_all, row_w, wg, wu, wd)


def _route_tables(route):
    n_tok = route.shape[0]
    n_asg = n_tok * TOP_K
    eid = route[:, 0:TOP_K].astype(jnp.int32).reshape(n_asg)
    wts = route[:, TOP_K:2 * TOP_K].reshape(n_asg)
    order = jnp.argsort(eid)
    eid_s = eid[order]
    counts = jnp.bincount(eid, length=N_EXPERTS)
    starts = jnp.cumsum(counts) - counts
    padded = (counts + MOE_BLOCK - 1) // MOE_BLOCK * MOE_BLOCK
    pad_ends = jnp.cumsum(padded)
    pad_starts = pad_ends - padded
    dest = pad_starts[eid_s] + jnp.arange(n_asg, dtype=jnp.int32) - starts[eid_s]
    n_blocks = -(-(n_asg + N_EXPERTS * (MOE_BLOCK - 1)) // MOE_BLOCK)
    n_rows = n_blocks * MOE_BLOCK
    code_s = ((order // TOP_K) * 4 + (order % TOP_K) * 2 + 1).astype(jnp.int32)
    code = jnp.zeros((n_rows,), jnp.int32).at[dest].set(code_s)
    row_w = jnp.zeros((n_rows,), F32).at[dest].set(wts[order])
    block_expert = jnp.minimum(
        jnp.searchsorted(pad_ends, jnp.arange(n_blocks, dtype=jnp.int32) * MOE_BLOCK, side='right'),
        N_EXPERTS - 1).astype(jnp.int32)
    n_used = (pad_ends[-1:] // MOE_BLOCK).astype(jnp.int32)
    return code, row_w.reshape(n_rows, 1), block_expert, n_used


def _final_kernel(x1_ref, y0_ref, y1_ref, mod_ref, ln_ref, o_ref):
    gb, ts, d = o_ref.shape
    moe = (y0_ref[...] + y1_ref[...]).reshape(gb, ts, d)
    h = DEEPNORM_ALPHA * x1_ref[...].reshape(gb, ts, d) + mod_ref[:, 5:6, :] * moe
    o_ref[...] = _layer_norm(h, ln_ref[0:1, :], ln_ref[1:2, :])


def _final(x1, y, mod, ln, b, s, gb, ts, tile0, n_tok):
    d = x1.shape[1]
    rows = gb * ts
    nj = s // ts
    k_off = n_tok // rows
    return pl.pallas_call(
        _final_kernel,
        grid=(b // gb, nj),
        in_specs=[pl.BlockSpec((rows, d), lambda i, j: (i * nj + j, 0)),
                  pl.BlockSpec((rows, d), lambda i, j: (tile0 + i * nj + j, 0)),
                  pl.BlockSpec((rows, d), lambda i, j: (k_off + tile0 + i * nj + j, 0)),
                  pl.BlockSpec((gb, 6, d), lambda i, j: (i, 0, 0)),
                  pl.BlockSpec(ln.shape, lambda i, j: (0, 0))],
        out_specs=pl.BlockSpec((gb, ts, d), lambda i, j: (i, j, 0)),
        out_shape=jax.ShapeDtypeStruct((b, s, d), F32),
        compiler_params=_params(("arbitrary", "arbitrary")),
    )(x1, y, y, mod, ln)


def _rope_tables(n_past, s):
    inv_freq = ROPE_THETA ** (-jnp.arange(0, MLA_ROPE, 2, dtype=F32) / MLA_ROPE)
    ang = jnp.arange(n_past, n_past + s).astype(F32)[:, None] * inv_freq[None, :]
    cos, sin = jnp.cos(ang), jnp.sin(ang)
    return jnp.concatenate([cos, cos, cos, cos], axis=1), jnp.concatenate([-sin, sin, -sin, sin], axis=1)


def _pad_cols(w, n):
    return jnp.pad(w, ((0, 0), (0, n - w.shape[1])))


def kernel(x_prompt, x_sample, c_prompt, c_sample, cache_kv_latent, cache_k_rope, state_shift, state_wkv, w_ada, b_ada, w_in, q_norm_g, w_uq, kv_norm_g, w_uk, w_uv, rwkv_mu, rwkv_w0, rwkv_w2, rwkv_a0, rwkv_a2, rwkv_g2, rwkv_k_k, rwkv_k_a, rwkv_r_k, rwkv_lnx_g, rwkv_lnx_b, w_out, ln1_g, ln1_b, router_group_w, router_group_b, router_expert_w, router_expert_b, expert_w_gate, expert_w_up, expert_w_down, ln2_g, ln2_b):
    depth = w_in.shape[0]
    assert depth == 1
    bp, sp, d = x_prompt.shape
    bs, ss, _ = x_sample.shape
    n_past = cache_kv_latent.shape[2]
    assert d == D_MODEL and sp % ROW_TILE == 0 and ss == CHUNK and bs % (ROW_TILE // CHUNK) == 0
    assert sp % ATT_TQ == 0 and n_past % ATT_TK_PAST == 0
    gb_s = ROW_TILE // ss
    mla_proj = Q_LORA + KV_LORA + MLA_ROPE

    wi = w_in[0]
    w_in_p = jnp.concatenate(
        [wi[:, :mla_proj], wi[:, mla_proj - MLA_ROPE:mla_proj], _pad_cols(wi[:, mla_proj:], RW_SLOT)],
        axis=1).astype(BF16)
    wq = w_uq[0].reshape(Q_LORA, MLA_HEADS, MLA_QK)
    wq_rope = wq[:, :, MLA_NOPE:]
    wuq_p = jnp.concatenate(
        [wq[:, :, :MLA_NOPE].reshape(Q_LORA, MLA_WIDTH),
         jnp.concatenate([wq_rope, wq_rope], axis=2).reshape(Q_LORA, MLA_HEADS * LANES)], axis=1).astype(BF16)
    wuk_t = jnp.transpose(w_uk[0], (1, 2, 0)).astype(BF16)
    wuv_t = jnp.transpose(w_uv[0], (1, 0, 2)).astype(BF16)
    gq = q_norm_g[0].reshape(1, Q_LORA)
    gkv = kv_norm_g[0].reshape(1, KV_LORA)
    mu_p = _pad_cols(rwkv_mu[0].reshape(1, RWKV_PROJ), RW_SLOT)
    zeros_c = jnp.zeros((RWKV_WIDTH,), F32)
    vecs = jnp.stack([rwkv_w0[0], rwkv_a0[0], rwkv_k_k[0], rwkv_k_a[0], rwkv_r_k[0].reshape(RWKV_WIDTH),
                      rwkv_lnx_g[0], rwkv_lnx_b[0], zeros_c])
    w2_p = jnp.concatenate([rwkv_w2[0], jnp.zeros((AAA_LORA, RWKV_WIDTH), F32)], axis=0).astype(BF16)
    a2_p = jnp.concatenate([jnp.zeros((DECAY_LORA, RWKV_WIDTH), F32), rwkv_a2[0]], axis=0).astype(BF16)
    g2_p = jnp.concatenate([rwkv_g2[0], jnp.zeros((RW_GATE_SLOT - GATE_LORA, RWKV_WIDTH), F32)],
                           axis=0).astype(BF16)
    wo1 = w_out[0][:MLA_WIDTH].astype(BF16)
    wo2 = w_out[0][MLA_WIDTH:].astype(BF16)
    ln1 = jnp.stack([ln1_g[0], ln1_b[0]])
    ln2 = jnp.stack([ln2_g[0], ln2_b[0]])
    wr = _pad_cols(jnp.concatenate([router_group_w[0], router_expert_w[0]], axis=1), LANES)
    wr3 = jnp.stack(_split3(wr))
    rb = _pad_cols(jnp.concatenate([router_group_b[0], router_expert_b[0]]).reshape(1, -1), LANES)

    mod = _ada(jnp.concatenate([c_prompt, c_sample], axis=0), w_ada[0], b_ada[0]).reshape(bp + bs, 6, d)
    mod_p, mod_s = mod[:bp], mod[bp:]

    def mix_group(x, mod_g, gb, ts, tq, tt, cache_kv, cache_kr, shift_prev, wkv_prev):
        b, s, _ = x.shape
        pm, prw = _inproj(x, mod_g, w_in_p, gb, ts)
        n_p = 0 if cache_kv is None else cache_kv.shape[1]
        cos4, sin4 = _rope_tables(n_p, s)
        mla, kv_new, kr_new = _attn(pm, cos4, sin4, gq, gkv, wuq_p, wuk_t, wuv_t, cache_kv, cache_kr, tq)
        shift_p = _pad_cols(shift_prev.reshape(b, RWKV_PROJ), RW_SLOT).reshape(b, 1, RW_SLOT)
        zt0 = jnp.swapaxes(wkv_prev.astype(F32), -1, -2)
        rwk, ztn = _rwkv(prw, shift_p, zt0, mu_p, vecs, w2_p, a2_p, g2_p, tt)
        x1, u2, route = _outproj(mla, rwk, x, mod_g, wo1, wo2, ln1, wr3, rb, gb, ts)
        shift_new = prw[:, s - 1:s, :RWKV_PROJ]
        return x1, u2, route, kv_new, kr_new, shift_new, jnp.swapaxes(ztn, -1, -2)

    zero_shift = jnp.zeros((bp, 1, RWKV_PROJ), F32)
    zero_wkv = jnp.zeros((bp, RWKV_HEADS, RWKV_HEAD, RWKV_HEAD), F32)
    x1_p, u2_p, route_p, kv_p, kr_p, sh_p, wkv_p = mix_group(
        x_prompt, mod_p, 1, ROW_TILE, ATT_TQ, RWKV_TT, None, None, zero_shift, zero_wkv)
    x1_s, u2_s, route_s, kv_s, kr_s, sh_s, wkv_s = mix_group(
        x_sample, mod_s, gb_s, ss, ss, ss, cache_kv_latent[0], cache_k_rope[0], state_shift[0], state_wkv[0])

    n_p_tok = bp * sp
    n_tok = n_p_tok + bs * ss
    code, row_w, block_expert, n_used = _route_tables(jnp.concatenate([route_p, route_s], axis=0))
    y = _moe(jnp.concatenate([u2_p, u2_s], axis=0), code, row_w, block_expert, n_used,
             expert_w_gate[0], expert_w_up[0], expert_w_down[0])

    out_p = _final(x1_p, y, mod_p, ln2, bp, sp, 1, ROW_TILE, 0, n_tok)
    out_s = _final(x1_s, y, mod_s, ln2, bs, ss, gb_s, ss, n_p_tok // ROW_TILE, n_tok)
    return (out_p, out_s, kv_p[None], kr_p[None], sh_p[None], wkv_p[None],
            kv_s[None], kr_s[None], sh_s[None], wkv_s[None])
```

```python
import functools

import jax
import jax.numpy as jnp
from jax import lax
from jax.experimental import pallas as pl
from jax.experimental.pallas import tpu as pltpu

F32 = jnp.float32
BF16 = jnp.bfloat16

LANES = 128
SUBLANES = 8

D_MODEL = 2048
CHUNK = 64
MLA_HEADS = 8
MLA_NOPE = 128
MLA_ROPE = 64
MLA_VDIM = 128
MLA_QK = MLA_NOPE + MLA_ROPE
MLA_WIDTH = MLA_HEADS * MLA_VDIM
Q_LORA = 512
KV_LORA = 512
RWKV_HEADS = 16
RWKV_HEAD = 64
RWKV_WIDTH = RWKV_HEADS * RWKV_HEAD
DECAY_LORA = 64
AAA_LORA = 64
GATE_LORA = 160
RWKV_PROJ = 3 * RWKV_WIDTH + DECAY_LORA + AAA_LORA + GATE_LORA
N_GROUPS = 8
EXPERTS_PER_GROUP = 8
N_EXPERTS = N_GROUPS * EXPERTS_PER_GROUP
TOP_K = 2
D_EXPERT = 512
MOE_BLOCK = 128
ROPE_THETA = 10000.0
LN_EPS = 1e-5
RMS_EPS = 1e-6
GN_EPS = 64e-5
DECAY_SCALE = 0.606531
SOFTMAX_SCALE = MLA_QK ** -0.5
DEEPNORM_ALPHA = 2.0 ** 0.25

MLA_SLOT = Q_LORA + KV_LORA + LANES
RW_LORA_SLOT = DECAY_LORA + AAA_LORA
RW_GATE_SLOT = 2 * LANES
RW_SLOT = 3 * RWKV_WIDTH + RW_LORA_SLOT + RW_GATE_SLOT
ROW_TILE = 512
ATT_TQ = 256
ATT_TK_PAST = 512
RWKV_TT = 128
PAIR = 2 * RWKV_HEAD
VMEM_LIMIT = 56 * 1024 * 1024


def _dot(a, b):
    return jnp.dot(a.astype(BF16), b.astype(BF16), preferred_element_type=F32)


def _dot_nt(a, b):
    return lax.dot_general(a.astype(BF16), b.astype(BF16), (((1,), (1,)), ((), ())),
                           preferred_element_type=F32)


def _dot_tn(a, b):
    return lax.dot_general(a.astype(BF16), b.astype(BF16), (((0,), (0,)), ((), ())),
                           preferred_element_type=F32)


def _split3(x):
    h = x.astype(BF16)
    r1 = x - h.astype(F32)
    m = r1.astype(BF16)
    l = (r1 - m.astype(F32)).astype(BF16)
    return h, m, l


def _iota(shape, dim):
    return lax.broadcasted_iota(jnp.int32, shape, dim)


def _params(sem):
    return pltpu.CompilerParams(dimension_semantics=sem, vmem_limit_bytes=VMEM_LIMIT)


def _ada_kernel(c_ref, w_ref, b_ref, o_ref):
    c = c_ref[...]
    s = c * jax.nn.sigmoid(c)
    o_ref[...] = _dot(s, w_ref[...]) + b_ref[...]


def _ada(c_all, w_ada, b_ada):
    nb, d = c_all.shape
    n = w_ada.shape[1]
    tn = 1536
    return pl.pallas_call(
        _ada_kernel,
        grid=(n // tn,),
        in_specs=[pl.BlockSpec((nb, d), lambda j: (0, 0)),
                  pl.BlockSpec((d, tn), lambda j: (0, j)),
                  pl.BlockSpec((1, tn), lambda j: (0, j))],
        out_specs=pl.BlockSpec((nb, tn), lambda j: (0, j)),
        out_shape=jax.ShapeDtypeStruct((nb, n), F32),
        compiler_params=_params(("arbitrary",)),
    )(c_all, w_ada, b_ada.reshape(1, n))


def _inproj_kernel(x_ref, mod_ref, w_ref, om_ref, or_ref, u_ref):
    n = pl.program_id(2)
    gb, ts, d = x_ref.shape

    @pl.when(n == 0)
    def _():
        u = x_ref[...] * (1.0 + mod_ref[:, 1:2, :]) + mod_ref[:, 0:1, :]
        u_ref[...] = u.reshape(gb * ts, d).astype(BF16)

    res = jnp.dot(u_ref[...], w_ref[...], preferred_element_type=F32)

    @pl.when(n == 0)
    def _():
        om_ref[...] = res.reshape(om_ref.shape)

    @pl.when(n > 0)
    def _():
        or_ref[...] = res.reshape(or_ref.shape)


def _inproj(x, mod, w_packed, gb, ts):
    b, s, d = x.shape
    tn = MLA_SLOT
    nt = w_packed.shape[1] // tn
    return pl.pallas_call(
        _inproj_kernel,
        grid=(b // gb, s // ts, nt),
        in_specs=[pl.BlockSpec((gb, ts, d), lambda i, j, n: (i, j, 0)),
                  pl.BlockSpec((gb, 6, d), lambda i, j, n: (i, 0, 0)),
                  pl.BlockSpec((d, tn), lambda i, j, n: (0, n))],
        out_specs=[pl.BlockSpec((gb, ts, tn), lambda i, j, n: (i, j, 0)),
                   pl.BlockSpec((gb, ts, tn), lambda i, j, n: (i, j, jnp.maximum(n - 1, 0)))],
        out_shape=[jax.ShapeDtypeStruct((b, s, MLA_SLOT), F32),
                   jax.ShapeDtypeStruct((b, s, RW_SLOT), F32)],
        scratch_shapes=[pltpu.VMEM((gb * ts, d), BF16)],
        compiler_params=_params(("arbitrary", "arbitrary", "arbitrary")),
    )(x, mod, w_packed)


def _rms(x, g):
    return x * lax.rsqrt(jnp.mean(jnp.square(x), axis=-1, keepdims=True) + RMS_EPS) * g


def _rope_slot(slot, cos4, sin4):
    return slot * cos4 + pltpu.roll(slot, MLA_ROPE // 2, axis=1) * sin4


def _attn_kernel(*refs, tq, n_past, tk_past):
    if n_past:
        (pm_ref, cos_ref, sin_ref, gq_ref, gkv_ref, wuq_ref, wuk_ref, wuv_ref, ckv_ref, ckr_ref,
         mla_ref, kv_ref, kr_ref, qlat_ref, qrope_ref, m_ref, l_ref, acc_ref) = refs
    else:
        (pm_ref, cos_ref, sin_ref, gq_ref, gkv_ref, wuq_ref, wuk_ref, wuv_ref,
         mla_ref, kv_ref, kr_ref, qlat_ref, qrope_ref, m_ref, l_ref, acc_ref, kvs_ref, krs_ref) = refs
    j = pl.program_id(1)
    rows = MLA_HEADS * tq
    p = pm_ref[0]
    cos4 = cos_ref[...]
    sin4 = sin_ref[...]

    kv = _rms(p[:, Q_LORA:Q_LORA + KV_LORA], gkv_ref[...])
    kr = _rope_slot(p[:, Q_LORA + KV_LORA:MLA_SLOT], cos4, sin4)[:, :MLA_ROPE]
    kv_ref[0] = kv
    kr_ref[0] = kr
    kv_b = kv.astype(BF16)
    kr_b = kr.astype(BF16)

    q = _dot(_rms(p[:, :Q_LORA], gq_ref[...]), wuq_ref[...])
    for h in range(MLA_HEADS):
        qlat = _dot(q[:, h * MLA_NOPE:(h + 1) * MLA_NOPE], wuk_ref[h])
        rot = _rope_slot(q[:, MLA_WIDTH + h * LANES:MLA_WIDTH + (h + 1) * LANES], cos4, sin4)
        qlat_ref[h * tq:(h + 1) * tq, :] = (qlat * SOFTMAX_SCALE).astype(BF16)
        qrope_ref[h * tq:(h + 1) * tq, :] = (rot[:, :MLA_ROPE] * SOFTMAX_SCALE).astype(BF16)

    m_ref[...] = jnp.full(m_ref.shape, -jnp.inf, F32)
    l_ref[...] = jnp.zeros(l_ref.shape, F32)
    acc_ref[...] = jnp.zeros(acc_ref.shape, F32)

    def flash_step(kvb, krb, mask):
        s = _dot_nt(qlat_ref[...], kvb) + _dot_nt(qrope_ref[...], krb)
        if mask is not None:
            s = jnp.where(mask, s, -jnp.inf)
        m_prev = m_ref[...]
        m_new = jnp.maximum(m_prev, jnp.max(s, axis=-1, keepdims=True))
        alpha = jnp.exp(m_prev - m_new)
        pr = jnp.exp(s - m_new)
        l_ref[...] = alpha * l_ref[...] + jnp.sum(pr, axis=-1, keepdims=True)
        acc_ref[...] = alpha * acc_ref[...] + _dot(pr, kvb)
        m_ref[...] = m_new

    if n_past:
        def past_body(kb, carry):
            off = pl.multiple_of(kb * tk_past, tk_past)
            flash_step(ckv_ref[0, pl.ds(off, tk_past), :].astype(BF16),
                       ckr_ref[0, pl.ds(off, tk_past), :].astype(BF16), None)
            return carry
        lax.fori_loop(0, n_past // tk_past, past_body, 0)
        flash_step(kv_b, kr_b, None)
    else:
        off_j = pl.multiple_of(j * tq, tq)
        kvs_ref[pl.ds(off_j, tq), :] = kv_b
        krs_ref[pl.ds(off_j, tq), :] = kr_b

        def prev_body(kb, carry):
            off = pl.multiple_of(kb * tq, tq)
            flash_step(kvs_ref[pl.ds(off, tq), :], krs_ref[pl.ds(off, tq), :], None)
            return carry
        lax.fori_loop(0, j, prev_body, 0)
        q_chunk = (_iota((rows, tq), 0) & (tq - 1)) >> 6
        k_chunk = _iota((rows, tq), 1) >> 6
        flash_step(kv_b, kr_b, k_chunk <= q_chunk)

    o = acc_ref[...] / l_ref[...]
    for h in range(MLA_HEADS):
        mla_ref[0, :, h * MLA_VDIM:(h + 1) * MLA_VDIM] = _dot(o[h * tq:(h + 1) * tq, :], wuv_ref[h]).astype(BF16)


def _attn(pm, cos4, sin4, gq, gkv, wuq_p, wuk_t, wuv_t, cache_kv, cache_kr, tq):
    b, s, _ = pm.shape
    n_past = 0 if cache_kv is None else cache_kv.shape[1]
    rows = MLA_HEADS * tq
    const2 = lambda i, j: (0, 0)
    const3 = lambda i, j: (0, 0, 0)
    in_specs = [pl.BlockSpec((1, tq, MLA_SLOT), lambda i, j: (i, j, 0)),
                pl.BlockSpec((tq, LANES), lambda i, j: (j, 0)),
                pl.BlockSpec((tq, LANES), lambda i, j: (j, 0)),
                pl.BlockSpec((1, Q_LORA), const2),
                pl.BlockSpec((1, KV_LORA), const2),
                pl.BlockSpec(wuq_p.shape, const2),
                pl.BlockSpec(wuk_t.shape, const3),
                pl.BlockSpec(wuv_t.shape, const3)]
    args = [pm, cos4, sin4, gq, gkv, wuq_p, wuk_t, wuv_t]
    scratch = [pltpu.VMEM((rows, KV_LORA), BF16), pltpu.VMEM((rows, MLA_ROPE), BF16),
               pltpu.VMEM((rows, 1), F32), pltpu.VMEM((rows, 1), F32), pltpu.VMEM((rows, KV_LORA), F32)]
    if n_past:
        in_specs += [pl.BlockSpec((1, n_past, KV_LORA), lambda i, j: (i, 0, 0)),
                     pl.BlockSpec((1, n_past, MLA_ROPE), lambda i, j: (i, 0, 0))]
        args += [cache_kv, cache_kr]
    else:
        scratch += [pltpu.VMEM((s, KV_LORA), BF16), pltpu.VMEM((s, MLA_ROPE), BF16)]
    return pl.pallas_call(
        functools.partial(_attn_kernel, tq=tq, n_past=n_past, tk_past=min(ATT_TK_PAST, max(n_past, 1))),
        grid=(b, s // tq),
        in_specs=in_specs,
        out_specs=[pl.BlockSpec((1, tq, MLA_WIDTH), lambda i, j: (i, j, 0)),
                   pl.BlockSpec((1, tq, KV_LORA), lambda i, j: (i, j, 0)),
                   pl.BlockSpec((1, tq, MLA_ROPE), lambda i, j: (i, j, 0))],
        out_shape=[jax.ShapeDtypeStruct((b, s, MLA_WIDTH), BF16),
                   jax.ShapeDtypeStruct((b, s, KV_LORA), F32),
                   jax.ShapeDtypeStruct((b, s, MLA_ROPE), F32)],
        scratch_shapes=scratch,
        compiler_params=_params(("arbitrary", "arbitrary")),
    )(*args)


def _rwkv_kernel(pr_ref, shift_ref, zt0_ref, mu_ref, vec_ref, w2_ref, a2_ref, g2_ref,
                 out_ref, ztn_ref,
                 z_ref, carry_ref, y_ref, kt_ref, rt_ref, kh_ref, bh_ref, kb_ref, bb_ref, v_ref, wc_ref, *, tt):
    j = pl.program_id(1)
    nj = pl.num_programs(1)
    c = RWKV_WIDTH
    n_pairs = RWKV_HEADS // 2
    n_chunks = tt // CHUNK

    bd_mask = (_iota((PAIR, PAIR), 0) >> 6) == (_iota((PAIR, PAIR), 1) >> 6)

    @pl.when(j == 0)
    def _():
        carry_ref[...] = jnp.broadcast_to(shift_ref[0], carry_ref.shape)
        for pr in range(n_pairs):
            z_ref[pr] = jnp.zeros((PAIR, PAIR), F32)
            z_ref[pr, 0:RWKV_HEAD, 0:RWKV_HEAD] = zt0_ref[0, 2 * pr]
            z_ref[pr, RWKV_HEAD:PAIR, RWKV_HEAD:PAIR] = zt0_ref[0, 2 * pr + 1]

    rw = pr_ref[0]
    row = _iota((tt, 1), 0)
    prev = jnp.where(row == 0, carry_ref[0:1, :], pltpu.roll(rw, 1, axis=0))
    carry_ref[...] = jnp.broadcast_to(rw[tt - 1:tt, :], carry_ref.shape)
    xm = rw + (prev - rw) * mu_ref[...]

    w0 = vec_ref[0:1, :]
    a0 = vec_ref[1:2, :]
    k_k = vec_ref[2:3, :]
    k_a = vec_ref[3:4, :]
    r_k = vec_ref[4:5, :]
    lnx_g = vec_ref[5:6, :]
    lnx_b = vec_ref[6:7, :]

    r = xm[:, 0:c]
    k = xm[:, c:2 * c]
    v = xm[:, 2 * c:3 * c]
    lora = xm[:, 3 * c:3 * c + RW_LORA_SLOT]
    gd = xm[:, 3 * c + RW_LORA_SLOT:RW_SLOT]
    lw = -DECAY_SCALE * jax.nn.sigmoid(w0 + _dot(jnp.tanh(lora), w2_ref[...]))
    a = jax.nn.sigmoid(a0 + _dot(lora, a2_ref[...]))
    g = _dot(jax.nn.sigmoid(gd), g2_ref[...])

    ones_bd = jnp.where(bd_mask, 1.0, 0.0).astype(BF16)

    def head_sum(x):
        h, m, _ = _split3(x)
        cols = []
        for blk in range(c // LANES):
            sl = slice(blk * LANES, (blk + 1) * LANES)
            cols.append(jnp.dot(h[:, sl], ones_bd, preferred_element_type=F32)
                        + jnp.dot(m[:, sl], ones_bd, preferred_element_type=F32))
        return jnp.concatenate(cols, axis=1)

    kk = k * k_k
    kk = kk / jnp.maximum(jnp.sqrt(head_sum(kk * kk)), 1e-12)
    k_mod = k * (1.0 + (a - 1.0) * k_a)
    bonus = head_sum(r * k_mod * r_k) * v
    b_vec = kk * a

    tri = ((_iota((tt, tt), 0) >> 6) == (_iota((tt, tt), 1) >> 6)) & (_iota((tt, tt), 1) <= _iota((tt, tt), 0))
    tri_b = jnp.where(tri, 1.0, 0.0).astype(BF16)
    lw_h, lw_m, lw_l = _split3(lw)
    cum = (jnp.dot(tri_b, lw_h, preferred_element_type=F32) + jnp.dot(tri_b, lw_m, preferred_element_type=F32)
           + jnp.dot(tri_b, lw_l, preferred_element_type=F32))
    cum_end = jnp.concatenate(
        [jnp.broadcast_to(cum[(ci + 1) * CHUNK - 1:(ci + 1) * CHUNK, :], (CHUNK, c)) for ci in range(n_chunks)],
        axis=0)
    for ci in range(n_chunks):
        wc_ref[ci * SUBLANES:(ci + 1) * SUBLANES, :] = jnp.broadcast_to(
            jnp.exp(cum[(ci + 1) * CHUNK - 1:(ci + 1) * CHUNK, :]), (SUBLANES, c))
    e_inv = jnp.exp(-cum)
    e_end = jnp.exp(cum_end - cum)
    kt_ref[...] = (kk * jnp.exp(cum - lw)).astype(BF16)
    rt_ref[...] = (r * jnp.exp(cum)).astype(BF16)
    kh_ref[...] = (k_mod * e_inv).astype(BF16)
    bh_ref[...] = (b_vec * e_inv).astype(BF16)
    kb_ref[...] = (k_mod * e_end).astype(BF16)
    bb_ref[...] = (b_vec * e_end).astype(BF16)
    v_ref[...] = v.astype(BF16)

    t_i = _iota((CHUNK, PAIR), 0)
    j_i = _iota((CHUNK, PAIR), 1) & (RWKV_HEAD - 1)
    strict2 = j_i < t_i
    incl2 = j_i <= t_i
    blk16 = (j_i >> 4) == (t_i >> 4)
    eye2 = jnp.where(j_i == t_i, 1.0, 0.0)
    zero_b = jnp.zeros((PAIR, PAIR), BF16)

    def stack2(x):
        xb = x.astype(BF16)
        return jnp.where(bd_mask, jnp.concatenate([xb, xb], axis=0), zero_b)

    def pmm(x, y):
        return jnp.dot(x.astype(BF16), stack2(y), preferred_element_type=F32)

    items = [(ci, pr) for ci in range(n_chunks) for pr in range(n_pairs)]
    n_it = len(items)
    rsl = lambda ci: slice(ci * CHUNK, (ci + 1) * CHUNK)
    csl = lambda pr: slice(pr * PAIR, (pr + 1) * PAIR)
    each = lambda fn: [fn(i) for i in range(n_it)]

    kt = [kt_ref[rsl(ci), csl(pr)] for ci, pr in items]
    rt = [rt_ref[rsl(ci), csl(pr)] for ci, pr in items]
    vv = [v_ref[rsl(ci), csl(pr)] for ci, pr in items]
    khbh = [jnp.concatenate([stack2(kh_ref[rsl(ci), csl(pr)]), stack2(bh_ref[rsl(ci), csl(pr)])], axis=0)
            for ci, pr in items]
    a4 = each(lambda i: _dot_nt(jnp.concatenate([kt[i], rt[i]], axis=0), khbh[i]))
    m_mat = each(lambda i: jnp.where(strict2, a4[i][:CHUNK, :PAIR], 0.0))
    n_mat = each(lambda i: jnp.where(strict2, a4[i][:CHUNK, PAIR:], 0.0))
    p_mat = each(lambda i: jnp.where(incl2, a4[i][CHUNK:, :PAIR], 0.0))
    q_mat = each(lambda i: jnp.where(incl2, a4[i][CHUNK:, PAIR:], 0.0))
    n_d = each(lambda i: jnp.where(blk16, n_mat[i], 0.0))
    n_o = each(lambda i: n_mat[i] - n_d[i])
    n2 = each(lambda i: pmm(n_d[i], n_d[i]))
    t1 = each(lambda i: pmm(eye2 - n_d[i], eye2 + n2[i]))
    n4 = each(lambda i: pmm(n2[i], n2[i]))
    t2 = each(lambda i: pmm(t1[i], eye2 + n4[i]))
    n8 = each(lambda i: pmm(n4[i], n4[i]))
    t_d = each(lambda i: pmm(t2[i], eye2 + n8[i]))
    x1 = each(lambda i: pmm(t_d[i], n_o[i]))
    x2 = each(lambda i: pmm(x1[i], x1[i]))
    t3 = each(lambda i: pmm(eye2 - x1[i], eye2 + x2[i]))
    t_mat = each(lambda i: pmm(t3[i], t_d[i]))
    mvpv = each(lambda i: jnp.dot(jnp.concatenate([m_mat[i], p_mat[i]], axis=0).astype(BF16), stack2(vv[i]),
                                  preferred_element_type=F32))
    tkmv = each(lambda i: jnp.dot(t_mat[i].astype(BF16),
                                  jnp.concatenate([stack2(kt[i]), stack2(mvpv[i][:CHUNK])], axis=1),
                                  preferred_element_type=F32))
    z = [z_ref[pr] for pr in range(n_pairs)]
    for ci in range(n_chunks):
        ids = [ci * n_pairs + pr for pr in range(n_pairs)]
        zr = [jnp.dot(jnp.concatenate([tkmv[i][:, :PAIR].astype(BF16), rt[i]], axis=0), z[pr].astype(BF16),
                      preferred_element_type=F32) for pr, i in enumerate(ids)]
        sk = [zr[pr][:CHUNK] + tkmv[i][:, PAIR:] for pr, i in enumerate(ids)]
        qs = [pmm(q_mat[i], sk[pr]) for pr, i in enumerate(ids)]
        upd = [_dot_tn(jnp.concatenate([kb_ref[rsl(ci), csl(pr)], bb_ref[rsl(ci), csl(pr)]], axis=0),
                       jnp.concatenate([vv[i], (-sk[pr]).astype(BF16)], axis=0)) for pr, i in enumerate(ids)]
        for pr, i in enumerate(ids):
            y_ref[rsl(ci), csl(pr)] = zr[pr][CHUNK:] + mvpv[i][CHUNK:] - qs[pr]
            wc_col = jnp.transpose(wc_ref[ci * SUBLANES:(ci + 1) * SUBLANES, csl(pr)])[:, 0:1]
            z[pr] = z[pr] * wc_col + jnp.where(bd_mask, upd[pr], 0.0)
    for pr in range(n_pairs):
        z_ref[pr] = z[pr]


    y = y_ref[...]
    mean = head_sum(y) * (1.0 / RWKV_HEAD)
    dlt = y - mean
    var = head_sum(dlt * dlt) * (1.0 / RWKV_HEAD)
    yn = dlt * lax.rsqrt(var + GN_EPS) * lnx_g + lnx_b
    out_ref[0] = ((yn + bonus) * g).astype(BF16)

    @pl.when(j == nj - 1)
    def _():
        for pr in range(n_pairs):
            ztn_ref[0, 2 * pr] = z_ref[pr, 0:RWKV_HEAD, 0:RWKV_HEAD]
            ztn_ref[0, 2 * pr + 1] = z_ref[pr, RWKV_HEAD:PAIR, RWKV_HEAD:PAIR]


def _rwkv(pr, shift_p, zt0, mu_p, vecs, w2_p, a2_p, g2_p, tt):
    b, s, _ = pr.shape
    c = RWKV_WIDTH
    const2 = lambda i, j: (0, 0)
    full = lambda shape: pl.BlockSpec(shape, const2)
    return pl.pallas_call(
        functools.partial(_rwkv_kernel, tt=tt),
        grid=(b, s // tt),
        in_specs=[pl.BlockSpec((1, tt, RW_SLOT), lambda i, j: (i, j, 0)),
                  pl.BlockSpec((1, 1, RW_SLOT), lambda i, j: (i, 0, 0)),
                  pl.BlockSpec((1, RWKV_HEADS, RWKV_HEAD, RWKV_HEAD), lambda i, j: (i, 0, 0, 0)),
                  full((1, RW_SLOT)), full((SUBLANES, c)),
                  full(w2_p.shape), full(a2_p.shape), full(g2_p.shape)],
        out_specs=[pl.BlockSpec((1, tt, c), lambda i, j: (i, j, 0)),
                   pl.BlockSpec((1, RWKV_HEADS, RWKV_HEAD, RWKV_HEAD), lambda i, j: (i, 0, 0, 0))],
        out_shape=[jax.ShapeDtypeStruct((b, s, c), BF16),
                   jax.ShapeDtypeStruct((b, RWKV_HEADS, RWKV_HEAD, RWKV_HEAD), F32)],
        scratch_shapes=[pltpu.VMEM((RWKV_HEADS // 2, PAIR, PAIR), F32),
                        pltpu.VMEM((SUBLANES, RW_SLOT), F32),
                        pltpu.VMEM((tt, c), F32)]
                       + [pltpu.VMEM((tt, c), BF16) for _ in range(7)]
                       + [pltpu.VMEM((SUBLANES * (tt // CHUNK), c), F32)],
        compiler_params=_params(("arbitrary", "arbitrary")),
    )(pr, shift_p, zt0, mu_p, vecs, w2_p, a2_p, g2_p)


def _layer_norm(h, g, b):
    mu = jnp.mean(h, axis=-1, keepdims=True)
    d = h - mu
    var = jnp.mean(d * d, axis=-1, keepdims=True)
    return d * lax.rsqrt(var + LN_EPS) * g + b


def _outproj_kernel(mla_ref, rwk_ref, x_ref, mod_ref, wo1_ref, wo2_ref, ln_ref, wr_ref, rb_ref,
                    x1_ref, u2_ref, route_ref, cnt_ref):
    gb, ts, d = x_ref.shape
    rows = gb * ts
    mix = (jnp.dot(mla_ref[...].reshape(rows, MLA_WIDTH), wo1_ref[...], preferred_element_type=F32)
           + jnp.dot(rwk_ref[...].reshape(rows, RWKV_WIDTH), wo2_ref[...], preferred_element_type=F32))
    h = DEEPNORM_ALPHA * x_ref[...] + mod_ref[:, 2:3, :] * mix.reshape(gb, ts, d)
    x1 = _layer_norm(h, ln_ref[0:1, :], ln_ref[1:2, :])
    u2 = (x1 * (1.0 + mod_ref[:, 4:5, :]) + mod_ref[:, 3:4, :]).reshape(rows, d)
    x1_ref[...] = x1.reshape(rows, d)
    u2_ref[...] = u2

    uh, um, ul = _split3(u2)
    lg = (jnp.dot(uh, wr_ref[0], preferred_element_type=F32) + jnp.dot(um, wr_ref[0], preferred_element_type=F32)
          + jnp.dot(uh, wr_ref[1], preferred_element_type=F32) + jnp.dot(ul, wr_ref[0], preferred_element_type=F32)
          + jnp.dot(um, wr_ref[1], preferred_element_type=F32) + jnp.dot(uh, wr_ref[2], preferred_element_type=F32)
          ) + rb_ref[...]
    lane = _iota((rows, LANES), 1).astype(F32)
    neg = -jnp.inf
    big = float(LANES)
    gl = jnp.where(lane < N_GROUPS, lg, neg)
    gmax = jnp.max(gl, axis=-1, keepdims=True)
    p_grp = 1.0 / jnp.sum(jnp.exp(gl - gmax), axis=-1, keepdims=True)
    grp = jnp.min(jnp.where(gl == gmax, lane, big), axis=-1, keepdims=True)
    lo = N_GROUPS + grp * EXPERTS_PER_GROUP
    el = jnp.where((lane >= lo) & (lane < lo + EXPERTS_PER_GROUP), lg, neg)
    v1 = jnp.max(el, axis=-1, keepdims=True)
    i1 = jnp.min(jnp.where(el == v1, lane, big), axis=-1, keepdims=True)
    el2 = jnp.where(lane == i1, neg, el)
    v2 = jnp.max(el2, axis=-1, keepdims=True)
    i2 = jnp.min(jnp.where(el2 == v2, lane, big), axis=-1, keepdims=True)
    e2 = jnp.exp(v2 - v1)
    den = 1.0 / (1.0 + e2)
    route_ref[...] = jnp.where(lane == 0, i1 - N_GROUPS,
                               jnp.where(lane == 1, i2 - N_GROUPS,
                                         jnp.where(lane == 2, p_grp * den,
                                                   jnp.where(lane == 3, p_grp * e2 * den, 0.0))))
    hits = jnp.where((lane == i1 - N_GROUPS) | (lane == i2 - N_GROUPS), 1.0, 0.0)
    cnt_ref[0] = jnp.broadcast_to(jnp.sum(hits, axis=0, keepdims=True), (SUBLANES, LANES))


def _outproj(mla, rwk, x, mod, wo1, wo2, ln, wr3, rb, gb, ts):
    b, s, d = x.shape
    rows = gb * ts
    nj = s // ts
    const2 = lambda i, j: (0, 0)
    flat = lambda i, j: (i * nj + j, 0)
    return pl.pallas_call(
        _outproj_kernel,
        grid=(b // gb, nj),
        in_specs=[pl.BlockSpec((gb, ts, MLA_WIDTH), lambda i, j: (i, j, 0)),
                  pl.BlockSpec((gb, ts, RWKV_WIDTH), lambda i, j: (i, j, 0)),
                  pl.BlockSpec((gb, ts, d), lambda i, j: (i, j, 0)),
                  pl.BlockSpec((gb, 6, d), lambda i, j: (i, 0, 0)),
                  pl.BlockSpec(wo1.shape, const2), pl.BlockSpec(wo2.shape, const2),
                  pl.BlockSpec(ln.shape, const2),
                  pl.BlockSpec(wr3.shape, lambda i, j: (0, 0, 0)),
                  pl.BlockSpec(rb.shape, const2)],
        out_specs=[pl.BlockSpec((rows, d), flat), pl.BlockSpec((rows, d), flat),
                   pl.BlockSpec((rows, LANES), flat),
                   pl.BlockSpec((1, SUBLANES, LANES), lambda i, j: (i * nj + j, 0, 0))],
        out_shape=[jax.ShapeDtypeStruct((b * s, d), F32), jax.ShapeDtypeStruct((b * s, d), F32),
                   jax.ShapeDtypeStruct((b * s, LANES), F32),
                   jax.ShapeDtypeStruct((b * s // rows, SUBLANES, LANES), F32)],
        compiler_params=_params(("arbitrary", "arbitrary")),
    )(mla, rwk, x, mod, wo1, wo2, ln, wr3, rb)


def _moe_kernel(bexp_ref, code_ref, nused_ref,
                u2_hbm, wg_ref, wu_ref, wd_ref,
                y_hbm,
                xbuf, ybuf, wgb, wub, wdb, gsem, ssem, *, n_tok):
    i = pl.program_id(0)
    nused = nused_ref[0]
    slot = i % 2

    def gather_copy(src_row, blk_slot, r):
        return pltpu.make_async_copy(u2_hbm.at[pl.ds(src_row, 1)], xbuf.at[blk_slot, pl.ds(r, 1)],
                                     gsem.at[blk_slot])

    def scatter_copy(dst_row, blk_slot, r):
        return pltpu.make_async_copy(ybuf.at[blk_slot, pl.ds(r, 1)], y_hbm.at[pl.ds(dst_row, 1)],
                                     ssem.at[blk_slot])

    def issue_gather(blk, blk_slot):
        def body(r, carry):
            gather_copy(code_ref[blk * MOE_BLOCK + r] >> 2, blk_slot, r).start()
            return carry
        lax.fori_loop(0, MOE_BLOCK, body, 0, unroll=8)

    def wait_gather(blk_slot):
        def body(r, carry):
            gather_copy(0, blk_slot, r).wait()
            return carry
        lax.fori_loop(0, MOE_BLOCK, body, 0, unroll=8)

    def scatter_rows(blk, blk_slot, start):
        def body(r, carry):
            code = code_ref[blk * MOE_BLOCK + r]

            @pl.when((code & 1) == 1)
            def _():
                cp = scatter_copy(((code >> 1) & 1) * n_tok + (code >> 2), blk_slot, r)
                if start:
                    cp.start()
                else:
                    cp.wait()
            return carry
        lax.fori_loop(0, MOE_BLOCK, body, 0, unroll=8)

    @pl.when(i < nused)
    def _():
        @pl.when(i == 0)
        def _():
            issue_gather(0, 0)

        @pl.when(i + 1 < nused)
        def _():
            issue_gather(i + 1, 1 - slot)

        wait_gather(slot)

        prev_e = bexp_ref[jnp.maximum(i - 1, 0)]

        @pl.when((i == 0) | (bexp_ref[i] != prev_e))
        def _():
            wgb[...] = wg_ref[0].astype(BF16)
            wub[...] = wu_ref[0].astype(BF16)
            wdb[...] = wd_ref[0].astype(BF16)

        xb = xbuf[slot].astype(BF16)
        hg = jnp.dot(xb, wgb[...], preferred_element_type=F32)
        hu = jnp.dot(xb, wub[...], preferred_element_type=F32)
        hh = (hg * jax.nn.sigmoid(hg) * hu).astype(BF16)
        yv = jnp.dot(hh, wdb[...], preferred_element_type=F32)

        @pl.when(i >= 2)
        def _():
            scatter_rows(i - 2, slot, False)

        ybuf[slot] = yv
        scatter_rows(i, slot, True)

        @pl.when(i == nused - 1)
        def _():
            scatter_rows(i, slot, False)

            @pl.when(i >= 1)
            def _():
                scatter_rows(i - 1, 1 - slot, False)


def _moe(u2_all, code, block_expert, n_used, wg, wu, wd):
    n_tok, d = u2_all.shape
    n_blocks = block_expert.shape[0]
    de = wg.shape[2]

    def wmap(i, bexp, code_r, nused):
        return (bexp[jnp.minimum(i, nused[0] - 1)], 0, 0)

    grid_spec = pltpu.PrefetchScalarGridSpec(
        num_scalar_prefetch=3,
        grid=(n_blocks,),
        in_specs=[pl.BlockSpec(memory_space=pl.ANY),
                  pl.BlockSpec((1, d, de), wmap),
                  pl.BlockSpec((1, d, de), wmap),
                  pl.BlockSpec((1, de, d), wmap)],
        out_specs=pl.BlockSpec(memory_space=pl.ANY),
        scratch_shapes=[pltpu.VMEM((2, MOE_BLOCK, d), F32), pltpu.VMEM((2, MOE_BLOCK, d), F32),
                        pltpu.VMEM((d, de), BF16), pltpu.VMEM((d, de), BF16), pltpu.VMEM((de, d), BF16),
                        pltpu.SemaphoreType.DMA((2,)), pltpu.SemaphoreType.DMA((2,))])
    return pl.pallas_call(
        functools.partial(_moe_kernel, n_tok=n_tok),
        grid_spec=grid_spec,
        out_shape=jax.ShapeDtypeStruct((TOP_K * n_tok, d), F32),
        compiler_params=_params(("arbitrary",)),
    )(block_expert, code, n_used, u2_all, wg, wu, wd)


def _rank_kernel(route_ref, base_ref, o_ref):
    rows = route_ref.shape[0]
    lane = _iota((rows, LANES), 1).astype(F32)
    hit1 = lane == route_ref[:, 0:1]
    hit2 = lane == route_ref[:, 1:2]
    hits = jnp.where(hit1 | hit2, 1.0, 0.0).astype(BF16)
    earlier = jnp.where(_iota((rows, rows), 1) < _iota((rows, rows), 0), 1.0, 0.0).astype(BF16)
    pos = jnp.dot(earlier, hits, preferred_element_type=F32) + base_ref[0, 0:1, :]
    d1 = jnp.sum(jnp.where(hit1, pos, 0.0), axis=-1, keepdims=True)
    d2 = jnp.sum(jnp.where(hit2, pos, 0.0), axis=-1, keepdims=True)
    o_ref[...] = jnp.where(lane == 0, d1, jnp.where(lane == 1, d2, 0.0))


def _route_tables(route, tile_counts):
    n_tok = route.shape[0]
    n_asg = n_tok * TOP_K
    n_tiles = tile_counts.shape[0]
    rows = n_tok // n_tiles
    tc = tile_counts[:, 0, :].astype(jnp.int32)
    counts = jnp.sum(tc, axis=0)
    padded = (counts + MOE_BLOCK - 1) // MOE_BLOCK * MOE_BLOCK
    pad_ends = jnp.cumsum(padded)
    pad_starts = pad_ends - padded
    base = (pad_starts[None, :] + jnp.cumsum(tc, axis=0) - tc).astype(F32)
    base = jnp.broadcast_to(base[:, None, :], (n_tiles, SUBLANES, LANES))
    dest = pl.pallas_call(
        _rank_kernel,
        grid=(n_tiles,),
        in_specs=[pl.BlockSpec((rows, LANES), lambda i: (i, 0)),
                  pl.BlockSpec((1, SUBLANES, LANES), lambda i: (i, 0, 0))],
        out_specs=pl.BlockSpec((rows, LANES), lambda i: (i, 0)),
        out_shape=jax.ShapeDtypeStruct((n_tok, LANES), F32),
        compiler_params=_params(("arbitrary",)),
    )(route, base)
    dest = dest[:, 0:TOP_K].astype(jnp.int32).reshape(n_asg)
    n_blocks = -(-(n_asg + N_EXPERTS * (MOE_BLOCK - 1)) // MOE_BLOCK)
    n_rows = n_blocks * MOE_BLOCK
    code = jnp.zeros((n_rows,), jnp.int32).at[dest].set(
        jnp.arange(n_asg, dtype=jnp.int32) * 2 + 1, unique_indices=True)
    blk_start = jnp.arange(n_blocks, dtype=jnp.int32) * MOE_BLOCK
    block_expert = jnp.minimum(
        jnp.sum((pad_ends[None, :N_EXPERTS] <= blk_start[:, None]).astype(jnp.int32), axis=1),
        N_EXPERTS - 1).astype(jnp.int32)
    n_used = (pad_ends[N_EXPERTS - 1:N_EXPERTS] // MOE_BLOCK).astype(jnp.int32)
    return code, block_expert, n_used


def _final_kernel(x1_ref, y0_ref, y1_ref, route_ref, mod_ref, ln_ref, o_ref):
    gb, ts, d = o_ref.shape
    moe = (route_ref[:, 2:3] * y0_ref[...] + route_ref[:, 3:4] * y1_ref[...]).reshape(gb, ts, d)
    h = DEEPNORM_ALPHA * x1_ref[...].reshape(gb, ts, d) + mod_ref[:, 5:6, :] * moe
    o_ref[...] = _layer_norm(h, ln_ref[0:1, :], ln_ref[1:2, :])


def _final(x1, y, route, mod, ln, b, s, gb, ts, tile0, n_tok):
    d = x1.shape[1]
    rows = gb * ts
    nj = s // ts
    k_off = n_tok // rows
    return pl.pallas_call(
        _final_kernel,
        grid=(b // gb, nj),
        in_specs=[pl.BlockSpec((rows, d), lambda i, j: (i * nj + j, 0)),
                  pl.BlockSpec((rows, d), lambda i, j: (tile0 + i * nj + j, 0)),
                  pl.BlockSpec((rows, d), lambda i, j: (k_off + tile0 + i * nj + j, 0)),
                  pl.BlockSpec((rows, LANES), lambda i, j: (i * nj + j, 0)),
                  pl.BlockSpec((gb, 6, d), lambda i, j: (i, 0, 0)),
                  pl.BlockSpec(ln.shape, lambda i, j: (0, 0))],
        out_specs=pl.BlockSpec((gb, ts, d), lambda i, j: (i, j, 0)),
        out_shape=jax.ShapeDtypeStruct((b, s, d), F32),
        compiler_params=_params(("arbitrary", "arbitrary")),
    )(x1, y, y, route, mod, ln)


def _rope_tables(n_past, s):
    inv_freq = ROPE_THETA ** (-jnp.arange(0, MLA_ROPE, 2, dtype=F32) / MLA_ROPE)
    ang = jnp.arange(n_past, n_past + s).astype(F32)[:, None] * inv_freq[None, :]
    cos, sin = jnp.cos(ang), jnp.sin(ang)
    return jnp.concatenate([cos, cos, cos, cos], axis=1), jnp.concatenate([-sin, sin, -sin, sin], axis=1)


def _pad_cols(w, n):
    return jnp.pad(w, ((0, 0), (0, n - w.shape[1])))


def kernel(x_prompt, x_sample, c_prompt, c_sample, cache_kv_latent, cache_k_rope, state_shift, state_wkv, w_ada, b_ada, w_in, q_norm_g, w_uq, kv_norm_g, w_uk, w_uv, rwkv_mu, rwkv_w0, rwkv_w2, rwkv_a0, rwkv_a2, rwkv_g2, rwkv_k_k, rwkv_k_a, rwkv_r_k, rwkv_lnx_g, rwkv_lnx_b, w_out, ln1_g, ln1_b, router_group_w, router_group_b, router_expert_w, router_expert_b, expert_w_gate, expert_w_up, expert_w_down, ln2_g, ln2_b):
    depth = w_in.shape[0]
    assert depth == 1
    bp, sp, d = x_prompt.shape
    bs, ss, _ = x_sample.shape
    n_past = cache_kv_latent.shape[2]
    assert d == D_MODEL and sp % ROW_TILE == 0 and ss == CHUNK and bs % (ROW_TILE // CHUNK) == 0
    assert sp % ATT_TQ == 0 and n_past % ATT_TK_PAST == 0
    gb_s = ROW_TILE // ss
    mla_proj = Q_LORA + KV_LORA + MLA_ROPE

    wi = w_in[0]
    w_in_p = jnp.concatenate(
        [wi[:, :mla_proj], wi[:, mla_proj - MLA_ROPE:mla_proj], _pad_cols(wi[:, mla_proj:], RW_SLOT)],
        axis=1).astype(BF16)
    wq = w_uq[0].reshape(Q_LORA, MLA_HEADS, MLA_QK)
    wq_rope = wq[:, :, MLA_NOPE:]
    wuq_p = jnp.concatenate(
        [wq[:, :, :MLA_NOPE].reshape(Q_LORA, MLA_WIDTH),
         jnp.concatenate([wq_rope, wq_rope], axis=2).reshape(Q_LORA, MLA_HEADS * LANES)], axis=1).astype(BF16)
    wuk_t = jnp.transpose(w_uk[0], (1, 2, 0)).astype(BF16)
    wuv_t = jnp.transpose(w_uv[0], (1, 0, 2)).astype(BF16)
    gq = q_norm_g[0].reshape(1, Q_LORA)
    gkv = kv_norm_g[0].reshape(1, KV_LORA)
    mu_p = _pad_cols(rwkv_mu[0].reshape(1, RWKV_PROJ), RW_SLOT)
    zeros_c = jnp.zeros((RWKV_WIDTH,), F32)
    vecs = jnp.stack([rwkv_w0[0], rwkv_a0[0], rwkv_k_k[0], rwkv_k_a[0], rwkv_r_k[0].reshape(RWKV_WIDTH),
                      rwkv_lnx_g[0], rwkv_lnx_b[0], zeros_c])
    w2_p = jnp.concatenate([rwkv_w2[0], jnp.zeros((AAA_LORA, RWKV_WIDTH), F32)], axis=0).astype(BF16)
    a2_p = jnp.concatenate([jnp.zeros((DECAY_LORA, RWKV_WIDTH), F32), rwkv_a2[0]], axis=0).astype(BF16)
    g2_p = jnp.concatenate([rwkv_g2[0], jnp.zeros((RW_GATE_SLOT - GATE_LORA, RWKV_WIDTH), F32)],
                           axis=0).astype(BF16)
    wo1 = w_out[0][:MLA_WIDTH].astype(BF16)
    wo2 = w_out[0][MLA_WIDTH:].astype(BF16)
    ln1 = jnp.stack([ln1_g[0], ln1_b[0]])
    ln2 = jnp.stack([ln2_g[0], ln2_b[0]])
    wr = _pad_cols(jnp.concatenate([router_group_w[0], router_expert_w[0]], axis=1), LANES)
    wr3 = jnp.stack(_split3(wr))
    rb = _pad_cols(jnp.concatenate([router_group_b[0], router_expert_b[0]]).reshape(1, -1), LANES)

    mod = _ada(jnp.concatenate([c_prompt, c_sample], axis=0), w_ada[0], b_ada[0]).reshape(bp + bs, 6, d)
    mod_p, mod_s = mod[:bp], mod[bp:]

    def mix_group(x, mod_g, gb, ts, tq, tt, cache_kv, cache_kr, shift_prev, wkv_prev):
        b, s, _ = x.shape
        pm, prw = _inproj(x, mod_g, w_in_p, gb, ts)
        n_p = 0 if cache_kv is None else cache_kv.shape[1]
        cos4, sin4 = _rope_tables(n_p, s)
        mla, kv_new, kr_new = _attn(pm, cos4, sin4, gq, gkv, wuq_p, wuk_t, wuv_t, cache_kv, cache_kr, tq)
        shift_p = _pad_cols(shift_prev.reshape(b, RWKV_PROJ), RW_SLOT).reshape(b, 1, RW_SLOT)
        zt0 = jnp.swapaxes(wkv_prev.astype(F32), -1, -2)
        rwk, ztn = _rwkv(prw, shift_p, zt0, mu_p, vecs, w2_p, a2_p, g2_p, tt)
        x1, u2, route, cnt = _outproj(mla, rwk, x, mod_g, wo1, wo2, ln1, wr3, rb, gb, ts)
        shift_new = prw[:, s - 1:s, :RWKV_PROJ]
        return x1, u2, (route, cnt), kv_new, kr_new, shift_new, jnp.swapaxes(ztn, -1, -2)

    zero_shift = jnp.zeros((bp, 1, RWKV_PROJ), F32)
    zero_wkv = jnp.zeros((bp, RWKV_HEADS, RWKV_HEAD, RWKV_HEAD), F32)
    x1_p, u2_p, route_p, kv_p, kr_p, sh_p, wkv_p = mix_group(
        x_prompt, mod_p, 1, ROW_TILE, ATT_TQ, RWKV_TT, None, None, zero_shift, zero_wkv)
    x1_s, u2_s, route_s, kv_s, kr_s, sh_s, wkv_s = mix_group(
        x_sample, mod_s, gb_s, ss, ss, ss, cache_kv_latent[0], cache_k_rope[0], state_shift[0], state_wkv[0])

    n_p_tok = bp * sp
    n_tok = n_p_tok + bs * ss
    code, block_expert, n_used = _route_tables(jnp.concatenate([route_p[0], route_s[0]], axis=0),
                                               jnp.concatenate([route_p[1], route_s[1]], axis=0))
    y = _moe(jnp.concatenate([u2_p, u2_s], axis=0), code, block_expert, n_used,
             expert_w_gate[0], expert_w_up[0], expert_w_down[0])

    out_p = _final(x1_p, y, route_p[0], mod_p, ln2, bp, sp, 1, ROW_TILE, 0, n_tok)
    out_s = _final(x1_s, y, route_s[0], mod_s, ln2, bs, ss, gb_s, ss, n_p_tok // ROW_TILE, n_tok)
    return (out_p, out_s, kv_p[None], kr_p[None], sh_p[None], wkv_p[None],
            kv_s[None], kr_s[None], sh_s[None], wkv_s[None])
```

```python
import functools

import jax
import jax.numpy as jnp
from jax import lax
from jax.experimental import pallas as pl
from jax.experimental.pallas import tpu as pltpu

F32 = jnp.float32
BF16 = jnp.bfloat16

LANES = 128
SUBLANES = 8

D_MODEL = 2048
CHUNK = 64
MLA_HEADS = 8
MLA_NOPE = 128
MLA_ROPE = 64
MLA_VDIM = 128
MLA_QK = MLA_NOPE + MLA_ROPE
MLA_WIDTH = MLA_HEADS * MLA_VDIM
Q_LORA = 512
KV_LORA = 512
RWKV_HEADS = 16
RWKV_HEAD = 64
RWKV_WIDTH = RWKV_HEADS * RWKV_HEAD
DECAY_LORA = 64
AAA_LORA = 64
GATE_LORA = 160
RWKV_PROJ = 3 * RWKV_WIDTH + DECAY_LORA + AAA_LORA + GATE_LORA
N_GROUPS = 8
EXPERTS_PER_GROUP = 8
N_EXPERTS = N_GROUPS * EXPERTS_PER_GROUP
TOP_K = 2
D_EXPERT = 512
MOE_BLOCK = 128
ROPE_THETA = 10000.0
LN_EPS = 1e-5
RMS_EPS = 1e-6
GN_EPS = 64e-5
DECAY_SCALE = 0.606531
SOFTMAX_SCALE = MLA_QK ** -0.5
DEEPNORM_ALPHA = 2.0 ** 0.25

MLA_SLOT = Q_LORA + KV_LORA + LANES
RW_LORA_SLOT = DECAY_LORA + AAA_LORA
RW_GATE_SLOT = 2 * LANES
RW_SLOT = 3 * RWKV_WIDTH + RW_LORA_SLOT + RW_GATE_SLOT
ROW_TILE = 512
ATT_TQ = 256
ATT_TK_PAST = 512
RWKV_TT = 128
PAIR = 2 * RWKV_HEAD
VMEM_LIMIT = 56 * 1024 * 1024


def _dot(a, b):
    return jnp.dot(a.astype(BF16), b.astype(BF16), preferred_element_type=F32)


def _dot_nt(a, b):
    return lax.dot_general(a.astype(BF16), b.astype(BF16), (((1,), (1,)), ((), ())),
                           preferred_element_type=F32)


def _dot_tn(a, b):
    return lax.dot_general(a.astype(BF16), b.astype(BF16), (((0,), (0,)), ((), ())),
                           preferred_element_type=F32)


def _split3(x):
    h = x.astype(BF16)
    r1 = x - h.astype(F32)
    m = r1.astype(BF16)
    l = (r1 - m.astype(F32)).astype(BF16)
    return h, m, l


def _iota(shape, dim):
    return lax.broadcasted_iota(jnp.int32, shape, dim)


def _params(sem):
    return pltpu.CompilerParams(dimension_semantics=sem, vmem_limit_bytes=VMEM_LIMIT)


def _ada_kernel(c_ref, w_ref, b_ref, o_ref):
    c = c_ref[...]
    s = c * jax.nn.sigmoid(c)
    o_ref[...] = _dot(s, w_ref[...]) + b_ref[...]


def _ada(c_all, w_ada, b_ada):
    nb, d = c_all.shape
    n = w_ada.shape[1]
    tn = 1536
    return pl.pallas_call(
        _ada_kernel,
        grid=(n // tn,),
        in_specs=[pl.BlockSpec((nb, d), lambda j: (0, 0)),
                  pl.BlockSpec((d, tn), lambda j: (0, j)),
                  pl.BlockSpec((1, tn), lambda j: (0, j))],
        out_specs=pl.BlockSpec((nb, tn), lambda j: (0, j)),
        out_shape=jax.ShapeDtypeStruct((nb, n), F32),
        compiler_params=_params(("arbitrary",)),
    )(c_all, w_ada, b_ada.reshape(1, n))


def _inproj_kernel(x_ref, mod_ref, w_ref, om_ref, or_ref, u_ref):
    n = pl.program_id(2)
    gb, ts, d = x_ref.shape

    @pl.when(n == 0)
    def _():
        u = x_ref[...] * (1.0 + mod_ref[:, 1:2, :]) + mod_ref[:, 0:1, :]
        u_ref[...] = u.reshape(gb * ts, d).astype(BF16)

    res = jnp.dot(u_ref[...], w_ref[...], preferred_element_type=F32)

    @pl.when(n == 0)
    def _():
        om_ref[...] = res.reshape(om_ref.shape)

    @pl.when(n > 0)
    def _():
        or_ref[...] = res.reshape(or_ref.shape)


def _inproj(x, mod, w_packed, gb, ts):
    b, s, d = x.shape
    tn = MLA_SLOT
    nt = w_packed.shape[1] // tn
    return pl.pallas_call(
        _inproj_kernel,
        grid=(b // gb, s // ts, nt),
        in_specs=[pl.BlockSpec((gb, ts, d), lambda i, j, n: (i, j, 0)),
                  pl.BlockSpec((gb, 6, d), lambda i, j, n: (i, 0, 0)),
                  pl.BlockSpec((d, tn), lambda i, j, n: (0, n))],
        out_specs=[pl.BlockSpec((gb, ts, tn), lambda i, j, n: (i, j, 0)),
                   pl.BlockSpec((gb, ts, tn), lambda i, j, n: (i, j, jnp.maximum(n - 1, 0)))],
        out_shape=[jax.ShapeDtypeStruct((b, s, MLA_SLOT), F32),
                   jax.ShapeDtypeStruct((b, s, RW_SLOT), F32)],
        scratch_shapes=[pltpu.VMEM((gb * ts, d), BF16)],
        compiler_params=_params(("arbitrary", "arbitrary", "arbitrary")),
    )(x, mod, w_packed)


def _rms(x, g):
    return x * lax.rsqrt(jnp.mean(jnp.square(x), axis=-1, keepdims=True) + RMS_EPS) * g


def _rope_slot(slot, cos4, sin4):
    return slot * cos4 + pltpu.roll(slot, MLA_ROPE // 2, axis=1) * sin4


def _attn_kernel(*refs, tq, n_past, tk_past):
    if n_past:
        (pm_ref, cos_ref, sin_ref, gq_ref, gkv_ref, wuq_ref, wuk_ref, wuv_ref, ckv_ref, ckr_ref,
         mla_ref, kv_ref, kr_ref, qlat_ref, qrope_ref, m_ref, l_ref, acc_ref) = refs
    else:
        (pm_ref, cos_ref, sin_ref, gq_ref, gkv_ref, wuq_ref, wuk_ref, wuv_ref,
         mla_ref, kv_ref, kr_ref, qlat_ref, qrope_ref, m_ref, l_ref, acc_ref, kvs_ref, krs_ref) = refs
    j = pl.program_id(1)
    rows = MLA_HEADS * tq
    p = pm_ref[0]
    cos4 = cos_ref[...]
    sin4 = sin_ref[...]

    kv = _rms(p[:, Q_LORA:Q_LORA + KV_LORA], gkv_ref[...])
    kr = _rope_slot(p[:, Q_LORA + KV_LORA:MLA_SLOT], cos4, sin4)[:, :MLA_ROPE]
    kv_ref[0] = kv
    kr_ref[0] = kr
    kv_b = kv.astype(BF16)
    kr_b = kr.astype(BF16)

    q = _dot(_rms(p[:, :Q_LORA], gq_ref[...]), wuq_ref[...])
    for h in range(MLA_HEADS):
        qlat = _dot(q[:, h * MLA_NOPE:(h + 1) * MLA_NOPE], wuk_ref[h])
        rot = _rope_slot(q[:, MLA_WIDTH + h * LANES:MLA_WIDTH + (h + 1) * LANES], cos4, sin4)
        qlat_ref[h * tq:(h + 1) * tq, :] = (qlat * SOFTMAX_SCALE).astype(BF16)
        qrope_ref[h * tq:(h + 1) * tq, :] = (rot[:, :MLA_ROPE] * SOFTMAX_SCALE).astype(BF16)

    m_ref[...] = jnp.full(m_ref.shape, -jnp.inf, F32)
    l_ref[...] = jnp.zeros(l_ref.shape, F32)
    acc_ref[...] = jnp.zeros(acc_ref.shape, F32)

    def flash_step(kvb, krb, mask):
        def scores(h):
            rs = slice(h * tq, (h + 1) * tq)
            return _dot_nt(qlat_ref[rs, :], kvb) + _dot_nt(qrope_ref[rs, :], krb)

        pending = None
        s_next = scores(0)
        for h in range(MLA_HEADS):
            rs = slice(h * tq, (h + 1) * tq)
            s = s_next
            if h + 1 < MLA_HEADS:
                s_next = scores(h + 1)
            if mask is not None:
                s = jnp.where(mask, s, -jnp.inf)
            m_prev = m_ref[rs, :]
            m_new = jnp.maximum(m_prev, jnp.max(s, axis=-1, keepdims=True))
            alpha = jnp.exp(m_prev - m_new)
            pr = jnp.exp(s - m_new)
            l_ref[rs, :] = alpha * l_ref[rs, :] + jnp.sum(pr, axis=-1, keepdims=True)
            m_ref[rs, :] = m_new
            pv = _dot(pr, kvb)
            if pending is not None:
                prs, palpha, ppv = pending
                acc_ref[prs, :] = palpha * acc_ref[prs, :] + ppv
            pending = (rs, alpha, pv)
        prs, palpha, ppv = pending
        acc_ref[prs, :] = palpha * acc_ref[prs, :] + ppv

    if n_past:
        def past_body(kb, carry):
            off = pl.multiple_of(kb * tk_past, tk_past)
            flash_step(ckv_ref[0, pl.ds(off, tk_past), :].astype(BF16),
                       ckr_ref[0, pl.ds(off, tk_past), :].astype(BF16), None)
            return carry
        lax.fori_loop(0, n_past // tk_past, past_body, 0)
        flash_step(kv_b, kr_b, None)
    else:
        off_j = pl.multiple_of(j * tq, tq)
        kvs_ref[pl.ds(off_j, tq), :] = kv_b
        krs_ref[pl.ds(off_j, tq), :] = kr_b

        def prev_body(kb, carry):
            off = pl.multiple_of(kb * tq, tq)
            flash_step(kvs_ref[pl.ds(off, tq), :], krs_ref[pl.ds(off, tq), :], None)
            return carry
        lax.fori_loop(0, j, prev_body, 0)
        q_chunk = _iota((tq, tq), 0) >> 6
        k_chunk = _iota((tq, tq), 1) >> 6
        flash_step(kv_b, kr_b, k_chunk <= q_chunk)

    o = acc_ref[...] / l_ref[...]
    for h in range(MLA_HEADS):
        mla_ref[0, :, h * MLA_VDIM:(h + 1) * MLA_VDIM] = _dot(o[h * tq:(h + 1) * tq, :], wuv_ref[h]).astype(BF16)


def _attn(pm, cos4, sin4, gq, gkv, wuq_p, wuk_t, wuv_t, cache_kv, cache_kr, tq):
    b, s, _ = pm.shape
    n_past = 0 if cache_kv is None else cache_kv.shape[1]
    rows = MLA_HEADS * tq
    const2 = lambda i, j: (0, 0)
    const3 = lambda i, j: (0, 0, 0)
    in_specs = [pl.BlockSpec((1, tq, MLA_SLOT), lambda i, j: (i, j, 0)),
                pl.BlockSpec((tq, LANES), lambda i, j: (j, 0)),
                pl.BlockSpec((tq, LANES), lambda i, j: (j, 0)),
                pl.BlockSpec((1, Q_LORA), const2),
                pl.BlockSpec((1, KV_LORA), const2),
                pl.BlockSpec(wuq_p.shape, const2),
                pl.BlockSpec(wuk_t.shape, const3),
                pl.BlockSpec(wuv_t.shape, const3)]
    args = [pm, cos4, sin4, gq, gkv, wuq_p, wuk_t, wuv_t]
    scratch = [pltpu.VMEM((rows, KV_LORA), BF16), pltpu.VMEM((rows, MLA_ROPE), BF16),
               pltpu.VMEM((rows, 1), F32), pltpu.VMEM((rows, 1), F32), pltpu.VMEM((rows, KV_LORA), F32)]
    if n_past:
        in_specs += [pl.BlockSpec((1, n_past, KV_LORA), lambda i, j: (i, 0, 0)),
                     pl.BlockSpec((1, n_past, MLA_ROPE), lambda i, j: (i, 0, 0))]
        args += [cache_kv, cache_kr]
    else:
        scratch += [pltpu.VMEM((s, KV_LORA), BF16), pltpu.VMEM((s, MLA_ROPE), BF16)]
    return pl.pallas_call(
        functools.partial(_attn_kernel, tq=tq, n_past=n_past, tk_past=min(ATT_TK_PAST, max(n_past, 1))),
        grid=(b, s // tq),
        in_specs=in_specs,
        out_specs=[pl.BlockSpec((1, tq, MLA_WIDTH), lambda i, j: (i, j, 0)),
                   pl.BlockSpec((1, tq, KV_LORA), lambda i, j: (i, j, 0)),
                   pl.BlockSpec((1, tq, MLA_ROPE), lambda i, j: (i, j, 0))],
        out_shape=[jax.ShapeDtypeStruct((b, s, MLA_WIDTH), BF16),
                   jax.ShapeDtypeStruct((b, s, KV_LORA), F32),
                   jax.ShapeDtypeStruct((b, s, MLA_ROPE), F32)],
        scratch_shapes=scratch,
        compiler_params=_params(("arbitrary", "arbitrary")),
    )(*args)


def _rwkv_kernel(pr_ref, shift_ref, zt0_ref, mu_ref, vec_ref, w2_ref, a2_ref, g2_ref,
                 out_ref, ztn_ref,
                 z_ref, carry_ref, y_ref, kt_ref, rt_ref, kh_ref, bh_ref, kb_ref, bb_ref, v_ref, wc_ref, *, tt):
    j = pl.program_id(1)
    nj = pl.num_programs(1)
    c = RWKV_WIDTH
    n_pairs = RWKV_HEADS // 2
    n_chunks = tt // CHUNK

    bd_mask = (_iota((PAIR, PAIR), 0) >> 6) == (_iota((PAIR, PAIR), 1) >> 6)

    @pl.when(j == 0)
    def _():
        carry_ref[...] = jnp.broadcast_to(shift_ref[0], carry_ref.shape)
        for pr in range(n_pairs):
            z_ref[pr] = jnp.zeros((PAIR, PAIR), F32)
            z_ref[pr, 0:RWKV_HEAD, 0:RWKV_HEAD] = zt0_ref[0, 2 * pr]
            z_ref[pr, RWKV_HEAD:PAIR, RWKV_HEAD:PAIR] = zt0_ref[0, 2 * pr + 1]

    rw = pr_ref[0]
    row = _iota((tt, 1), 0)
    prev = jnp.where(row == 0, carry_ref[0:1, :], pltpu.roll(rw, 1, axis=0))
    carry_ref[...] = jnp.broadcast_to(rw[tt - 1:tt, :], carry_ref.shape)
    xm = rw + (prev - rw) * mu_ref[...]

    w0 = vec_ref[0:1, :]
    a0 = vec_ref[1:2, :]
    k_k = vec_ref[2:3, :]
    k_a = vec_ref[3:4, :]
    r_k = vec_ref[4:5, :]
    lnx_g = vec_ref[5:6, :]
    lnx_b = vec_ref[6:7, :]

    r = xm[:, 0:c]
    k = xm[:, c:2 * c]
    v = xm[:, 2 * c:3 * c]
    lora = xm[:, 3 * c:3 * c + RW_LORA_SLOT]
    gd = xm[:, 3 * c + RW_LORA_SLOT:RW_SLOT]
    lw = -DECAY_SCALE * jax.nn.sigmoid(w0 + _dot(jnp.tanh(lora), w2_ref[...]))
    a = jax.nn.sigmoid(a0 + _dot(lora, a2_ref[...]))
    g = _dot(jax.nn.sigmoid(gd), g2_ref[...])

    ones_bd = jnp.where(bd_mask, 1.0, 0.0).astype(BF16)

    def head_sum(x):
        h, m, _ = _split3(x)
        cols = []
        for blk in range(c // LANES):
            sl = slice(blk * LANES, (blk + 1) * LANES)
            cols.append(jnp.dot(h[:, sl], ones_bd, preferred_element_type=F32)
                        + jnp.dot(m[:, sl], ones_bd, preferred_element_type=F32))
        return jnp.concatenate(cols, axis=1)

    kk = k * k_k
    kk = kk / jnp.maximum(jnp.sqrt(head_sum(kk * kk)), 1e-12)
    k_mod = k * (1.0 + (a - 1.0) * k_a)
    bonus = head_sum(r * k_mod * r_k) * v
    b_vec = kk * a

    tri = ((_iota((tt, tt), 0) >> 6) == (_iota((tt, tt), 1) >> 6)) & (_iota((tt, tt), 1) <= _iota((tt, tt), 0))
    tri_b = jnp.where(tri, 1.0, 0.0).astype(BF16)
    lw_h, lw_m, lw_l = _split3(lw)
    cum = (jnp.dot(tri_b, lw_h, preferred_element_type=F32) + jnp.dot(tri_b, lw_m, preferred_element_type=F32)
           + jnp.dot(tri_b, lw_l, preferred_element_type=F32))
    cum_end = jnp.concatenate(
        [jnp.broadcast_to(cum[(ci + 1) * CHUNK - 1:(ci + 1) * CHUNK, :], (CHUNK, c)) for ci in range(n_chunks)],
        axis=0)
    for ci in range(n_chunks):
        wc_ref[ci * SUBLANES:(ci + 1) * SUBLANES, :] = jnp.broadcast_to(
            jnp.exp(cum[(ci + 1) * CHUNK - 1:(ci + 1) * CHUNK, :]), (SUBLANES, c))
    e_inv = jnp.exp(-cum)
    e_end = jnp.exp(cum_end - cum)
    kt_ref[...] = (kk * jnp.exp(cum - lw)).astype(BF16)
    rt_ref[...] = (r * jnp.exp(cum)).astype(BF16)
    kh_ref[...] = (k_mod * e_inv).astype(BF16)
    bh_ref[...] = (b_vec * e_inv).astype(BF16)
    kb_ref[...] = (k_mod * e_end).astype(BF16)
    bb_ref[...] = (b_vec * e_end).astype(BF16)
    v_ref[...] = v.astype(BF16)

    t_i = _iota((CHUNK, PAIR), 0)
    j_i = _iota((CHUNK, PAIR), 1) & (RWKV_HEAD - 1)
    strict2 = j_i < t_i
    incl2 = j_i <= t_i
    blk16 = (j_i >> 4) == (t_i >> 4)
    eye2 = jnp.where(j_i == t_i, 1.0, 0.0)
    zero_b = jnp.zeros((PAIR, PAIR), BF16)

    def stack2(x):
        xb = x.astype(BF16)
        return jnp.where(bd_mask, jnp.concatenate([xb, xb], axis=0), zero_b)

    def pmm(x, y):
        return jnp.dot(x.astype(BF16), stack2(y), preferred_element_type=F32)

    items = [(ci, pr) for ci in range(n_chunks) for pr in range(n_pairs)]
    n_it = len(items)
    rsl = lambda ci: slice(ci * CHUNK, (ci + 1) * CHUNK)
    csl = lambda pr: slice(pr * PAIR, (pr + 1) * PAIR)
    each = lambda fn: [fn(i) for i in range(n_it)]

    kt = [kt_ref[rsl(ci), csl(pr)] for ci, pr in items]
    rt = [rt_ref[rsl(ci), csl(pr)] for ci, pr in items]
    vv = [v_ref[rsl(ci), csl(pr)] for ci, pr in items]
    khbh = [jnp.concatenate([stack2(kh_ref[rsl(ci), csl(pr)]), stack2(bh_ref[rsl(ci), csl(pr)])], axis=0)
            for ci, pr in items]
    a4 = each(lambda i: _dot_nt(jnp.concatenate([kt[i], rt[i]], axis=0), khbh[i]))
    m_mat = each(lambda i: jnp.where(strict2, a4[i][:CHUNK, :PAIR], 0.0))
    n_mat = each(lambda i: jnp.where(strict2, a4[i][:CHUNK, PAIR:], 0.0))
    p_mat = each(lambda i: jnp.where(incl2, a4[i][CHUNK:, :PAIR], 0.0))
    q_mat = each(lambda i: jnp.where(incl2, a4[i][CHUNK:, PAIR:], 0.0))
    n_d = each(lambda i: jnp.where(blk16, n_mat[i], 0.0))
    n_o = each(lambda i: n_mat[i] - n_d[i])
    n2 = each(lambda i: pmm(n_d[i], n_d[i]))
    t1 = each(lambda i: pmm(eye2 - n_d[i], eye2 + n2[i]))
    n4 = each(lambda i: pmm(n2[i], n2[i]))
    t2 = each(lambda i: pmm(t1[i], eye2 + n4[i]))
    n8 = each(lambda i: pmm(n4[i], n4[i]))
    t_d = each(lambda i: pmm(t2[i], eye2 + n8[i]))
    x1 = each(lambda i: pmm(t_d[i], n_o[i]))
    x2 = each(lambda i: pmm(x1[i], x1[i]))
    t3 = each(lambda i: pmm(eye2 - x1[i], eye2 + x2[i]))
    t_mat = each(lambda i: pmm(t3[i], t_d[i]))
    mvpv = each(lambda i: jnp.dot(jnp.concatenate([m_mat[i], p_mat[i]], axis=0).astype(BF16), stack2(vv[i]),
                                  preferred_element_type=F32))
    tkmv = each(lambda i: jnp.dot(t_mat[i].astype(BF16),
                                  jnp.concatenate([stack2(kt[i]), stack2(mvpv[i][:CHUNK])], axis=1),
                                  preferred_element_type=F32))
    z = [z_ref[pr] for pr in range(n_pairs)]
    for ci in range(n_chunks):
        ids = [ci * n_pairs + pr for pr in range(n_pairs)]
        zr = [jnp.dot(jnp.concatenate([tkmv[i][:, :PAIR].astype(BF16), rt[i]], axis=0), z[pr].astype(BF16),
                      preferred_element_type=F32) for pr, i in enumerate(ids)]
        sk = [zr[pr][:CHUNK] + tkmv[i][:, PAIR:] for pr, i in enumerate(ids)]
        qs = [pmm(q_mat[i], sk[pr]) for pr, i in enumerate(ids)]
        upd = [_dot_tn(jnp.concatenate([kb_ref[rsl(ci), csl(pr)], bb_ref[rsl(ci), csl(pr)]], axis=0),
                       jnp.concatenate([vv[i], (-sk[pr]).astype(BF16)], axis=0)) for pr, i in enumerate(ids)]
        for pr, i in enumerate(ids):
            y_ref[rsl(ci), csl(pr)] = zr[pr][CHUNK:] + mvpv[i][CHUNK:] - qs[pr]
            wc_col = jnp.transpose(wc_ref[ci * SUBLANES:(ci + 1) * SUBLANES, csl(pr)])[:, 0:1]
            z[pr] = z[pr] * wc_col + jnp.where(bd_mask, upd[pr], 0.0)
    for pr in range(n_pairs):
        z_ref[pr] = z[pr]


    y = y_ref[...]
    mean = head_sum(y) * (1.0 / RWKV_HEAD)
    dlt = y - mean
    var = head_sum(dlt * dlt) * (1.0 / RWKV_HEAD)
    yn = dlt * lax.rsqrt(var + GN_EPS) * lnx_g + lnx_b
    out_ref[0] = ((yn + bonus) * g).astype(BF16)

    @pl.when(j == nj - 1)
    def _():
        for pr in range(n_pairs):
            ztn_ref[0, 2 * pr] = z_ref[pr, 0:RWKV_HEAD, 0:RWKV_HEAD]
            ztn_ref[0, 2 * pr + 1] = z_ref[pr, RWKV_HEAD:PAIR, RWKV_HEAD:PAIR]


def _rwkv(pr, shift_p, zt0, mu_p, vecs, w2_p, a2_p, g2_p, tt):
    b, s, _ = pr.shape
    c = RWKV_WIDTH
    const2 = lambda i, j: (0, 0)
    full = lambda shape: pl.BlockSpec(shape, const2)
    return pl.pallas_call(
        functools.partial(_rwkv_kernel, tt=tt),
        grid=(b, s // tt),
        in_specs=[pl.BlockSpec((1, tt, RW_SLOT), lambda i, j: (i, j, 0)),
                  pl.BlockSpec((1, 1, RW_SLOT), lambda i, j: (i, 0, 0)),
                  pl.BlockSpec((1, RWKV_HEADS, RWKV_HEAD, RWKV_HEAD), lambda i, j: (i, 0, 0, 0)),
                  full((1, RW_SLOT)), full((SUBLANES, c)),
                  full(w2_p.shape), full(a2_p.shape), full(g2_p.shape)],
        out_specs=[pl.BlockSpec((1, tt, c), lambda i, j: (i, j, 0)),
                   pl.BlockSpec((1, RWKV_HEADS, RWKV_HEAD, RWKV_HEAD), lambda i, j: (i, 0, 0, 0))],
        out_shape=[jax.ShapeDtypeStruct((b, s, c), BF16),
                   jax.ShapeDtypeStruct((b, RWKV_HEADS, RWKV_HEAD, RWKV_HEAD), F32)],
        scratch_shapes=[pltpu.VMEM((RWKV_HEADS // 2, PAIR, PAIR), F32),
                        pltpu.VMEM((SUBLANES, RW_SLOT), F32),
                        pltpu.VMEM((tt, c), F32)]
                       + [pltpu.VMEM((tt, c), BF16) for _ in range(7)]
                       + [pltpu.VMEM((SUBLANES * (tt // CHUNK), c), F32)],
        compiler_params=_params(("arbitrary", "arbitrary")),
    )(pr, shift_p, zt0, mu_p, vecs, w2_p, a2_p, g2_p)


def _layer_norm(h, g, b):
    mu = jnp.mean(h, axis=-1, keepdims=True)
    d = h - mu
    var = jnp.mean(d * d, axis=-1, keepdims=True)
    return d * lax.rsqrt(var + LN_EPS) * g + b


def _outproj_kernel(mla_ref, rwk_ref, x_ref, mod_ref, wo1_ref, wo2_ref, ln_ref, wr_ref, rb_ref,
                    x1_ref, u2_ref, route_ref, cnt_ref):
    gb, ts, d = x_ref.shape
    rows = gb * ts
    mix = (jnp.dot(mla_ref[...].reshape(rows, MLA_WIDTH), wo1_ref[...], preferred_element_type=F32)
           + jnp.dot(rwk_ref[...].reshape(rows, RWKV_WIDTH), wo2_ref[...], preferred_element_type=F32))
    h = DEEPNORM_ALPHA * x_ref[...] + mod_ref[:, 2:3, :] * mix.reshape(gb, ts, d)
    x1 = _layer_norm(h, ln_ref[0:1, :], ln_ref[1:2, :])
    u2 = (x1 * (1.0 + mod_ref[:, 4:5, :]) + mod_ref[:, 3:4, :]).reshape(rows, d)
    x1_ref[...] = x1.reshape(rows, d)
    u2_ref[...] = u2

    uh, um, _ = _split3(u2)
    lg = (jnp.dot(uh, wr_ref[0], preferred_element_type=F32) + jnp.dot(um, wr_ref[0], preferred_element_type=F32)
          + jnp.dot(uh, wr_ref[1], preferred_element_type=F32)) + rb_ref[...]
    lane = _iota((rows, LANES), 1).astype(F32)
    neg = -jnp.inf
    big = float(LANES)
    gl = jnp.where(lane < N_GROUPS, lg, neg)
    gmax = jnp.max(gl, axis=-1, keepdims=True)
    p_grp = 1.0 / jnp.sum(jnp.exp(gl - gmax), axis=-1, keepdims=True)
    grp = jnp.min(jnp.where(gl == gmax, lane, big), axis=-1, keepdims=True)
    lo = N_GROUPS + grp * EXPERTS_PER_GROUP
    el = jnp.where((lane >= lo) & (lane < lo + EXPERTS_PER_GROUP), lg, neg)
    v1 = jnp.max(el, axis=-1, keepdims=True)
    i1 = jnp.min(jnp.where(el == v1, lane, big), axis=-1, keepdims=True)
    el2 = jnp.where(lane == i1, neg, el)
    v2 = jnp.max(el2, axis=-1, keepdims=True)
    i2 = jnp.min(jnp.where(el2 == v2, lane, big), axis=-1, keepdims=True)
    e2 = jnp.exp(v2 - v1)
    den = 1.0 / (1.0 + e2)
    route_ref[...] = jnp.where(lane == 0, i1 - N_GROUPS,
                               jnp.where(lane == 1, i2 - N_GROUPS,
                                         jnp.where(lane == 2, p_grp * den,
                                                   jnp.where(lane == 3, p_grp * e2 * den, 0.0))))
    hits = jnp.where((lane == i1 - N_GROUPS) | (lane == i2 - N_GROUPS), 1.0, 0.0)
    cnt_ref[0] = jnp.broadcast_to(jnp.sum(hits, axis=0, keepdims=True), (SUBLANES, LANES))


def _outproj(mla, rwk, x, mod, wo1, wo2, ln, wr3, rb, gb, ts):
    b, s, d = x.shape
    rows = gb * ts
    nj = s // ts
    const2 = lambda i, j: (0, 0)
    flat = lambda i, j: (i * nj + j, 0)
    return pl.pallas_call(
        _outproj_kernel,
        grid=(b // gb, nj),
        in_specs=[pl.BlockSpec((gb, ts, MLA_WIDTH), lambda i, j: (i, j, 0)),
                  pl.BlockSpec((gb, ts, RWKV_WIDTH), lambda i, j: (i, j, 0)),
                  pl.BlockSpec((gb, ts, d), lambda i, j: (i, j, 0)),
                  pl.BlockSpec((gb, 6, d), lambda i, j: (i, 0, 0)),
                  pl.BlockSpec(wo1.shape, const2), pl.BlockSpec(wo2.shape, const2),
                  pl.BlockSpec(ln.shape, const2),
                  pl.BlockSpec(wr3.shape, lambda i, j: (0, 0, 0)),
                  pl.BlockSpec(rb.shape, const2)],
        out_specs=[pl.BlockSpec((rows, d), flat), pl.BlockSpec((rows, d), flat),
                   pl.BlockSpec((rows, LANES), flat),
                   pl.BlockSpec((1, SUBLANES, LANES), lambda i, j: (i * nj + j, 0, 0))],
        out_shape=[jax.ShapeDtypeStruct((b * s, d), F32), jax.ShapeDtypeStruct((b * s, d), F32),
                   jax.ShapeDtypeStruct((b * s, LANES), F32),
                   jax.ShapeDtypeStruct((b * s // rows, SUBLANES, LANES), F32)],
        compiler_params=_params(("arbitrary", "arbitrary")),
    )(mla, rwk, x, mod, wo1, wo2, ln, wr3, rb)


def _moe_kernel(bexp_ref, code_ref, nval_ref, nused_ref,
                u2_hbm, wg_ref, wu_ref, wd_ref,
                y_hbm,
                xbuf, ybuf, wgb, wub, wdb, gsem, ssem, *, n_tok):
    i = pl.program_id(0)
    nused = nused_ref[0]
    slot = i % 2

    def gather_copy(src_row, xs, r):
        return pltpu.make_async_copy(u2_hbm.at[pl.ds(src_row, 1)], xbuf.at[xs, pl.ds(r, 1)], gsem.at[xs])

    def scatter_copy(blk, r, for_wait=False):
        if for_wait:
            dst = 0
        else:
            code = code_ref[blk * MOE_BLOCK + r]
            dst = ((code >> 1) & 1) * n_tok + (code >> 2)
        return pltpu.make_async_copy(ybuf.at[blk % 3, pl.ds(r, 1)], y_hbm.at[pl.ds(dst, 1)], ssem.at[blk % 3])

    def issue_gather(blk, xs):
        for r in range(MOE_BLOCK):
            gather_copy(code_ref[blk * MOE_BLOCK + r] >> 2, xs, r).start()

    def wait_gather(xs):
        def body(r, carry):
            gather_copy(0, xs, r).wait()
            return carry
        lax.fori_loop(0, MOE_BLOCK, body, 0, unroll=8)

    def scatter_partial(blk, start):
        def body(r, carry):
            if start:
                scatter_copy(blk, r).start()
            else:
                scatter_copy(blk, r, for_wait=True).wait()
            return carry
        lax.fori_loop(0, nval_ref[blk], body, 0)

    def wait_scatter(blk):
        @pl.when(nval_ref[blk] == MOE_BLOCK)
        def _():
            def body(r, carry):
                scatter_copy(blk, r, for_wait=True).wait()
                return carry
            lax.fori_loop(0, MOE_BLOCK, body, 0, unroll=8)

        @pl.when(nval_ref[blk] < MOE_BLOCK)
        def _():
            scatter_partial(blk, False)

    def compute(with_prev_scatter):
        issue_gather(jnp.minimum(i + 1, nused - 1), 1 - slot)
        if with_prev_scatter:
            for r in range(MOE_BLOCK):
                scatter_copy(i - 1, r).start()
        xb = xbuf[slot].astype(BF16)
        hg = jnp.dot(xb, wgb[...], preferred_element_type=F32)
        hu = jnp.dot(xb, wub[...], preferred_element_type=F32)
        hh = (hg * jax.nn.sigmoid(hg) * hu).astype(BF16)
        ybuf[i % 3] = jnp.dot(hh, wdb[...], preferred_element_type=F32)

    @pl.when(i == 0)
    def _():
        issue_gather(0, 0)

    @pl.when(i <= nused)
    def _():
        wait_gather(slot)

        @pl.when(i >= 3)
        def _():
            wait_scatter(i - 3)

    @pl.when(i < nused)
    def _():
        prev_e = bexp_ref[jnp.maximum(i - 1, 0)]

        @pl.when((i == 0) | (bexp_ref[i] != prev_e))
        def _():
            wgb[...] = wg_ref[0].astype(BF16)
            wub[...] = wu_ref[0].astype(BF16)
            wdb[...] = wd_ref[0].astype(BF16)

        prev_full = (i >= 1) & (nval_ref[jnp.maximum(i - 1, 0)] == MOE_BLOCK)

        @pl.when(prev_full)
        def _():
            compute(True)

        @pl.when(jnp.logical_not(prev_full))
        def _():
            compute(False)

            @pl.when(i >= 1)
            def _():
                scatter_partial(i - 1, True)

    @pl.when(i == nused)
    def _():
        scatter_partial(i - 1, True)

        @pl.when(i >= 2)
        def _():
            wait_scatter(i - 2)
        wait_scatter(i - 1)


def _moe(u2_all, code, n_valid, block_expert, n_used, wg, wu, wd):
    n_tok, d = u2_all.shape
    n_blocks = block_expert.shape[0]
    de = wg.shape[2]

    def wmap(i, bexp, code_r, nval, nused):
        return (bexp[jnp.minimum(i, nused[0] - 1)], 0, 0)

    grid_spec = pltpu.PrefetchScalarGridSpec(
        num_scalar_prefetch=4,
        grid=(n_blocks + 1,),
        in_specs=[pl.BlockSpec(memory_space=pl.ANY),
                  pl.BlockSpec((1, d, de), wmap),
                  pl.BlockSpec((1, d, de), wmap),
                  pl.BlockSpec((1, de, d), wmap)],
        out_specs=pl.BlockSpec(memory_space=pl.ANY),
        scratch_shapes=[pltpu.VMEM((2, MOE_BLOCK, d), F32), pltpu.VMEM((3, MOE_BLOCK, d), F32),
                        pltpu.VMEM((d, de), BF16), pltpu.VMEM((d, de), BF16), pltpu.VMEM((de, d), BF16),
                        pltpu.SemaphoreType.DMA((2,)), pltpu.SemaphoreType.DMA((3,))])
    return pl.pallas_call(
        functools.partial(_moe_kernel, n_tok=n_tok),
        grid_spec=grid_spec,
        out_shape=jax.ShapeDtypeStruct((TOP_K * n_tok, d), F32),
        compiler_params=_params(("arbitrary",)),
    )(block_expert, code, n_valid, n_used, u2_all, wg, wu, wd)


def _rank_kernel(route_ref, base_ref, o_ref):
    rows = route_ref.shape[0]
    lane = _iota((rows, LANES), 1).astype(F32)
    hit1 = lane == route_ref[:, 0:1]
    hit2 = lane == route_ref[:, 1:2]
    hits = jnp.where(hit1 | hit2, 1.0, 0.0).astype(BF16)
    earlier = jnp.where(_iota((rows, rows), 1) < _iota((rows, rows), 0), 1.0, 0.0).astype(BF16)
    pos = jnp.dot(earlier, hits, preferred_element_type=F32) + base_ref[0, 0:1, :]
    d1 = jnp.sum(jnp.where(hit1, pos, 0.0), axis=-1, keepdims=True)
    d2 = jnp.sum(jnp.where(hit2, pos, 0.0), axis=-1, keepdims=True)
    o_ref[...] = jnp.where(lane == 0, d1, jnp.where(lane == 1, d2, 0.0))


def _route_tables(route, tile_counts):
    n_tok = route.shape[0]
    n_asg = n_tok * TOP_K
    n_tiles = tile_counts.shape[0]
    rows = n_tok // n_tiles
    tc = tile_counts[:, 0, :].astype(jnp.int32)
    counts = jnp.sum(tc, axis=0)
    padded = (counts + MOE_BLOCK - 1) // MOE_BLOCK * MOE_BLOCK
    pad_ends = jnp.cumsum(padded)
    pad_starts = pad_ends - padded
    base = (pad_starts[None, :] + jnp.cumsum(tc, axis=0) - tc).astype(F32)
    base = jnp.broadcast_to(base[:, None, :], (n_tiles, SUBLANES, LANES))
    dest = pl.pallas_call(
        _rank_kernel,
        grid=(n_tiles,),
        in_specs=[pl.BlockSpec((rows, LANES), lambda i: (i, 0)),
                  pl.BlockSpec((1, SUBLANES, LANES), lambda i: (i, 0, 0))],
        out_specs=pl.BlockSpec((rows, LANES), lambda i: (i, 0)),
        out_shape=jax.ShapeDtypeStruct((n_tok, LANES), F32),
        compiler_params=_params(("arbitrary",)),
    )(route, base)
    dest = dest[:, 0:TOP_K].astype(jnp.int32).reshape(n_asg)
    n_blocks = -(-(n_asg + N_EXPERTS * (MOE_BLOCK - 1)) // MOE_BLOCK)
    n_rows = n_blocks * MOE_BLOCK
    code = jnp.zeros((n_rows,), jnp.int32).at[dest].set(
        jnp.arange(n_asg, dtype=jnp.int32) * 2 + 1, unique_indices=True)
    blk_start = jnp.arange(n_blocks, dtype=jnp.int32) * MOE_BLOCK
    block_expert = jnp.minimum(
        jnp.sum((pad_ends[None, :N_EXPERTS] <= blk_start[:, None]).astype(jnp.int32), axis=1),
        N_EXPERTS - 1).astype(jnp.int32)
    n_used = (pad_ends[N_EXPERTS - 1:N_EXPERTS] // MOE_BLOCK).astype(jnp.int32)
    filled = (pad_starts + counts)[:N_EXPERTS]
    n_valid = jnp.where(blk_start < pad_ends[N_EXPERTS - 1],
                        jnp.clip(filled[block_expert] - blk_start, 0, MOE_BLOCK), 0).astype(jnp.int32)
    return code, n_valid, block_expert, n_used


def _final_kernel(x1_ref, y0_ref, y1_ref, route_ref, mod_ref, ln_ref, o_ref):
    gb, ts, d = o_ref.shape
    moe = (route_ref[:, 2:3] * y0_ref[...] + route_ref[:, 3:4] * y1_ref[...]).reshape(gb, ts, d)
    h = DEEPNORM_ALPHA * x1_ref[...].reshape(gb, ts, d) + mod_ref[:, 5:6, :] * moe
    o_ref[...] = _layer_norm(h, ln_ref[0:1, :], ln_ref[1:2, :])


def _final(x1, y, route, mod, ln, b, s, gb, ts, tile0, n_tok):
    d = x1.shape[1]
    rows = gb * ts
    nj = s // ts
    k_off = n_tok // rows
    return pl.pallas_call(
        _final_kernel,
        grid=(b // gb, nj),
        in_specs=[pl.BlockSpec((rows, d), lambda i, j: (i * nj + j, 0)),
                  pl.BlockSpec((rows, d), lambda i, j: (tile0 + i * nj + j, 0)),
                  pl.BlockSpec((rows, d), lambda i, j: (k_off + tile0 + i * nj + j, 0)),
                  pl.BlockSpec((rows, LANES), lambda i, j: (i * nj + j, 0)),
                  pl.BlockSpec((gb, 6, d), lambda i, j: (i, 0, 0)),
                  pl.BlockSpec(ln.shape, lambda i, j: (0, 0))],
        out_specs=pl.BlockSpec((gb, ts, d), lambda i, j: (i, j, 0)),
        out_shape=jax.ShapeDtypeStruct((b, s, d), F32),
        compiler_params=_params(("arbitrary", "arbitrary")),
    )(x1, y, y, route, mod, ln)


def _rope_tables(n_past, s):
    inv_freq = ROPE_THETA ** (-jnp.arange(0, MLA_ROPE, 2, dtype=F32) / MLA_ROPE)
    ang = jnp.arange(n_past, n_past + s).astype(F32)[:, None] * inv_freq[None, :]
    cos, sin = jnp.cos(ang), jnp.sin(ang)
    return jnp.concatenate([cos, cos, cos, cos], axis=1), jnp.concatenate([-sin, sin, -sin, sin], axis=1)


def _pad_cols(w, n):
    return jnp.pad(w, ((0, 0), (0, n - w.shape[1])))


def kernel(x_prompt, x_sample, c_prompt, c_sample, cache_kv_latent, cache_k_rope, state_shift, state_wkv, w_ada, b_ada, w_in, q_norm_g, w_uq, kv_norm_g, w_uk, w_uv, rwkv_mu, rwkv_w0, rwkv_w2, rwkv_a0, rwkv_a2, rwkv_g2, rwkv_k_k, rwkv_k_a, rwkv_r_k, rwkv_lnx_g, rwkv_lnx_b, w_out, ln1_g, ln1_b, router_group_w, router_group_b, router_expert_w, router_expert_b, expert_w_gate, expert_w_up, expert_w_down, ln2_g, ln2_b):
    depth = w_in.shape[0]
    assert depth == 1
    bp, sp, d = x_prompt.shape
    bs, ss, _ = x_sample.shape
    n_past = cache_kv_latent.shape[2]
    assert d == D_MODEL and sp % ROW_TILE == 0 and ss == CHUNK and bs % (ROW_TILE // CHUNK) == 0
    assert sp % ATT_TQ == 0 and n_past % ATT_TK_PAST == 0
    gb_s = ROW_TILE // ss
    mla_proj = Q_LORA + KV_LORA + MLA_ROPE

    wi = w_in[0]
    w_in_p = jnp.concatenate(
        [wi[:, :mla_proj], wi[:, mla_proj - MLA_ROPE:mla_proj], _pad_cols(wi[:, mla_proj:], RW_SLOT)],
        axis=1).astype(BF16)
    wq = w_uq[0].reshape(Q_LORA, MLA_HEADS, MLA_QK)
    wq_rope = wq[:, :, MLA_NOPE:]
    wuq_p = jnp.concatenate(
        [wq[:, :, :MLA_NOPE].reshape(Q_LORA, MLA_WIDTH),
         jnp.concatenate([wq_rope, wq_rope], axis=2).reshape(Q_LORA, MLA_HEADS * LANES)], axis=1).astype(BF16)
    wuk_t = jnp.transpose(w_uk[0], (1, 2, 0)).astype(BF16)
    wuv_t = jnp.transpose(w_uv[0], (1, 0, 2)).astype(BF16)
    gq = q_norm_g[0].reshape(1, Q_LORA)
    gkv = kv_norm_g[0].reshape(1, KV_LORA)
    mu_p = _pad_cols(rwkv_mu[0].reshape(1, RWKV_PROJ), RW_SLOT)
    zeros_c = jnp.zeros((RWKV_WIDTH,), F32)
    vecs = jnp.stack([rwkv_w0[0], rwkv_a0[0], rwkv_k_k[0], rwkv_k_a[0], rwkv_r_k[0].reshape(RWKV_WIDTH),
                      rwkv_lnx_g[0], rwkv_lnx_b[0], zeros_c])
    w2_p = jnp.concatenate([rwkv_w2[0], jnp.zeros((AAA_LORA, RWKV_WIDTH), F32)], axis=0).astype(BF16)
    a2_p = jnp.concatenate([jnp.zeros((DECAY_LORA, RWKV_WIDTH), F32), rwkv_a2[0]], axis=0).astype(BF16)
    g2_p = jnp.concatenate([rwkv_g2[0], jnp.zeros((RW_GATE_SLOT - GATE_LORA, RWKV_WIDTH), F32)],
                           axis=0).astype(BF16)
    wo1 = w_out[0][:MLA_WIDTH].astype(BF16)
    wo2 = w_out[0][MLA_WIDTH:].astype(BF16)
    ln1 = jnp.stack([ln1_g[0], ln1_b[0]])
    ln2 = jnp.stack([ln2_g[0], ln2_b[0]])
    wr = _pad_cols(jnp.concatenate([router_group_w[0], router_expert_w[0]], axis=1), LANES)
    wr3 = jnp.stack(_split3(wr))
    rb = _pad_cols(jnp.concatenate([router_group_b[0], router_expert_b[0]]).reshape(1, -1), LANES)

    mod = _ada(jnp.concatenate([c_prompt, c_sample], axis=0), w_ada[0], b_ada[0]).reshape(bp + bs, 6, d)
    mod_p, mod_s = mod[:bp], mod[bp:]

    def mix_group(x, mod_g, gb, ts, tq, tt, cache_kv, cache_kr, shift_prev, wkv_prev):
        b, s, _ = x.shape
        pm, prw = _inproj(x, mod_g, w_in_p, gb, ts)
        n_p = 0 if cache_kv is None else cache_kv.shape[1]
        cos4, sin4 = _rope_tables(n_p, s)
        mla, kv_new, kr_new = _attn(pm, cos4, sin4, gq, gkv, wuq_p, wuk_t, wuv_t, cache_kv, cache_kr, tq)
        shift_p = _pad_cols(shift_prev.reshape(b, RWKV_PROJ), RW_SLOT).reshape(b, 1, RW_SLOT)
        zt0 = jnp.swapaxes(wkv_prev.astype(F32), -1, -2)
        rwk, ztn = _rwkv(prw, shift_p, zt0, mu_p, vecs, w2_p, a2_p, g2_p, tt)
        x1, u2, route, cnt = _outproj(mla, rwk, x, mod_g, wo1, wo2, ln1, wr3, rb, gb, ts)
        shift_new = prw[:, s - 1:s, :RWKV_PROJ]
        return x1, u2, (route, cnt), kv_new, kr_new, shift_new, jnp.swapaxes(ztn, -1, -2)

    zero_shift = jnp.zeros((bp, 1, RWKV_PROJ), F32)
    zero_wkv = jnp.zeros((bp, RWKV_HEADS, RWKV_HEAD, RWKV_HEAD), F32)
    x1_p, u2_p, route_p, kv_p, kr_p, sh_p, wkv_p = mix_group(
        x_prompt, mod_p, 1, ROW_TILE, ATT_TQ, RWKV_TT, None, None, zero_shift, zero_wkv)
    x1_s, u2_s, route_s, kv_s, kr_s, sh_s, wkv_s = mix_group(
        x_sample, mod_s, gb_s, ss, ss, ss, cache_kv_latent[0], cache_k_rope[0], state_shift[0], state_wkv[0])

    n_p_tok = bp * sp
    n_tok = n_p_tok + bs * ss
    code, n_valid, block_expert, n_used = _route_tables(jnp.concatenate([route_p[0], route_s[0]], axis=0),
                                                        jnp.concatenate([route_p[1], route_s[1]], axis=0))
    y = _moe(jnp.concatenate([u2_p, u2_s], axis=0), code, n_valid, block_expert, n_used,
             expert_w_gate[0], expert_w_up[0], expert_w_down[0])

    out_p = _final(x1_p, y, route_p[0], mod_p, ln2, bp, sp, 1, ROW_TILE, 0, n_tok)
    out_s = _final(x1_s, y, route_s[0], mod_s, ln2, bs, ss, gb_s, ss, n_p_tok // ROW_TILE, n_tok)
    return (out_p, out_s, kv_p[None], kr_p[None], sh_p[None], wkv_p[None],
            kv_s[None], kr_s[None], sh_s[None], wkv_s[None])
```

```python
import functools

import jax
import jax.numpy as jnp
from jax import lax
from jax.experimental import pallas as pl
from jax.experimental.pallas import tpu as pltpu

F32 = jnp.float32
BF16 = jnp.bfloat16

LANES = 128
SUBLANES = 8

D_MODEL = 2048
CHUNK = 64
MLA_HEADS = 8
MLA_NOPE = 128
MLA_ROPE = 64
MLA_VDIM = 128
MLA_QK = MLA_NOPE + MLA_ROPE
MLA_WIDTH = MLA_HEADS * MLA_VDIM
Q_LORA = 512
KV_LORA = 512
RWKV_HEADS = 16
RWKV_HEAD = 64
RWKV_WIDTH = RWKV_HEADS * RWKV_HEAD
DECAY_LORA = 64
AAA_LORA = 64
GATE_LORA = 160
RWKV_PROJ = 3 * RWKV_WIDTH + DECAY_LORA + AAA_LORA + GATE_LORA
N_GROUPS = 8
EXPERTS_PER_GROUP = 8
N_EXPERTS = N_GROUPS * EXPERTS_PER_GROUP
TOP_K = 2
D_EXPERT = 512
MOE_BLOCK = 128
ROPE_THETA = 10000.0
LN_EPS = 1e-5
RMS_EPS = 1e-6
GN_EPS = 64e-5
DECAY_SCALE = 0.606531
SOFTMAX_SCALE = MLA_QK ** -0.5
DEEPNORM_ALPHA = 2.0 ** 0.25

MLA_SLOT = Q_LORA + KV_LORA + LANES
RW_LORA_SLOT = DECAY_LORA + AAA_LORA
RW_GATE_SLOT = 2 * LANES
RW_SLOT = 3 * RWKV_WIDTH + RW_LORA_SLOT + RW_GATE_SLOT
ROW_TILE = 512
ATT_TQ = 256
ATT_TK_PAST = 256
RWKV_TT = 128
PAIR = 2 * RWKV_HEAD
VMEM_LIMIT = 56 * 1024 * 1024


def _dot(a, b):
    return jnp.dot(a.astype(BF16), b.astype(BF16), preferred_element_type=F32)


def _dot_nt(a, b):
    return lax.dot_general(a.astype(BF16), b.astype(BF16), (((1,), (1,)), ((), ())),
                           preferred_element_type=F32)


def _dot_tn(a, b):
    return lax.dot_general(a.astype(BF16), b.astype(BF16), (((0,), (0,)), ((), ())),
                           preferred_element_type=F32)


def _split3(x):
    h = x.astype(BF16)
    r1 = x - h.astype(F32)
    m = r1.astype(BF16)
    l = (r1 - m.astype(F32)).astype(BF16)
    return h, m, l


def _iota(shape, dim):
    return lax.broadcasted_iota(jnp.int32, shape, dim)


def _params(sem):
    return pltpu.CompilerParams(dimension_semantics=sem, vmem_limit_bytes=VMEM_LIMIT)


def _ada_kernel(c_ref, w_ref, b_ref, o_ref):
    c = c_ref[...]
    s = c * jax.nn.sigmoid(c)
    o_ref[...] = _dot(s, w_ref[...]) + b_ref[...]


def _ada(c_all, w_ada, b_ada):
    nb, d = c_all.shape
    n = w_ada.shape[1]
    tn = 1536
    return pl.pallas_call(
        _ada_kernel,
        grid=(n // tn,),
        in_specs=[pl.BlockSpec((nb, d), lambda j: (0, 0)),
                  pl.BlockSpec((d, tn), lambda j: (0, j)),
                  pl.BlockSpec((1, tn), lambda j: (0, j))],
        out_specs=pl.BlockSpec((nb, tn), lambda j: (0, j)),
        out_shape=jax.ShapeDtypeStruct((nb, n), F32),
        compiler_params=_params(("arbitrary",)),
    )(c_all, w_ada, b_ada.reshape(1, n))


def _inproj_kernel(x_ref, mod_ref, w_ref, om_ref, or_ref, u_ref):
    n = pl.program_id(2)
    gb, ts, d = x_ref.shape

    @pl.when(n == 0)
    def _():
        u = x_ref[...] * (1.0 + mod_ref[:, 1:2, :]) + mod_ref[:, 0:1, :]
        u_ref[...] = u.reshape(gb * ts, d).astype(BF16)

    res = jnp.dot(u_ref[...], w_ref[...], preferred_element_type=F32)

    @pl.when(n == 0)
    def _():
        om_ref[...] = res.reshape(om_ref.shape)

    @pl.when(n > 0)
    def _():
        or_ref[...] = res.reshape(or_ref.shape)


def _inproj(x, mod, w_packed, gb, ts):
    b, s, d = x.shape
    tn = MLA_SLOT
    nt = w_packed.shape[1] // tn
    return pl.pallas_call(
        _inproj_kernel,
        grid=(b // gb, s // ts, nt),
        in_specs=[pl.BlockSpec((gb, ts, d), lambda i, j, n: (i, j, 0)),
                  pl.BlockSpec((gb, 6, d), lambda i, j, n: (i, 0, 0)),
                  pl.BlockSpec((d, tn), lambda i, j, n: (0, n))],
        out_specs=[pl.BlockSpec((gb, ts, tn), lambda i, j, n: (i, j, 0)),
                   pl.BlockSpec((gb, ts, tn), lambda i, j, n: (i, j, jnp.maximum(n - 1, 0)))],
        out_shape=[jax.ShapeDtypeStruct((b, s, MLA_SLOT), F32),
                   jax.ShapeDtypeStruct((b, s, RW_SLOT), F32)],
        scratch_shapes=[pltpu.VMEM((gb * ts, d), BF16)],
        compiler_params=_params(("arbitrary", "arbitrary", "arbitrary")),
    )(x, mod, w_packed)


def _rms(x, g):
    return x * lax.rsqrt(jnp.mean(jnp.square(x), axis=-1, keepdims=True) + RMS_EPS) * g


def _rope_slot(slot, cos4, sin4):
    return slot * cos4 + pltpu.roll(slot, MLA_ROPE // 2, axis=1) * sin4


def _attn_kernel(*refs, tq, n_past, tk_past):
    if n_past:
        (pm_ref, cos_ref, sin_ref, gq_ref, gkv_ref, wuq_ref, wuk_ref, wuv_ref, ckv_ref, ckr_ref,
         mla_ref, kv_ref, kr_ref, qlat_ref, qrope_ref, m_ref, l_ref, acc_ref) = refs
    else:
        (pm_ref, cos_ref, sin_ref, gq_ref, gkv_ref, wuq_ref, wuk_ref, wuv_ref,
         mla_ref, kv_ref, kr_ref, qlat_ref, qrope_ref, m_ref, l_ref, acc_ref, kvs_ref, krs_ref) = refs
    j = pl.program_id(1)
    rows = MLA_HEADS * tq
    p = pm_ref[0]
    cos4 = cos_ref[...]
    sin4 = sin_ref[...]

    kv = _rms(p[:, Q_LORA:Q_LORA + KV_LORA], gkv_ref[...])
    kr = _rope_slot(p[:, Q_LORA + KV_LORA:MLA_SLOT], cos4, sin4)[:, :MLA_ROPE]
    kv_ref[0] = kv
    kr_ref[0] = kr
    kv_b = kv.astype(BF16)
    kr_b = kr.astype(BF16)

    q = _dot(_rms(p[:, :Q_LORA], gq_ref[...]), wuq_ref[...])
    for h in range(MLA_HEADS):
        qlat = _dot(q[:, h * MLA_NOPE:(h + 1) * MLA_NOPE], wuk_ref[h])
        rot = _rope_slot(q[:, MLA_WIDTH + h * LANES:MLA_WIDTH + (h + 1) * LANES], cos4, sin4)
        qlat_ref[h * tq:(h + 1) * tq, :] = (qlat * SOFTMAX_SCALE).astype(BF16)
        qrope_ref[h * tq:(h + 1) * tq, :] = (rot[:, :MLA_ROPE] * SOFTMAX_SCALE).astype(BF16)

    m_ref[...] = jnp.full(m_ref.shape, -jnp.inf, F32)
    l_ref[...] = jnp.zeros(l_ref.shape, F32)
    acc_ref[...] = jnp.zeros(acc_ref.shape, F32)

    grp_rows = max(tq, ATT_TQ)
    n_grp = rows // grp_rows

    def flash_step(kvb, krb, mask, kr_transposed=False):
        def scores(h):
            rs = slice(h * grp_rows, (h + 1) * grp_rows)
            rope = _dot(qrope_ref[rs, :], krb) if kr_transposed else _dot_nt(qrope_ref[rs, :], krb)
            return _dot_nt(qlat_ref[rs, :], kvb) + rope

        pending = None
        s_next = scores(0)
        for h in range(n_grp):
            rs = slice(h * grp_rows, (h + 1) * grp_rows)
            s = s_next
            if h + 1 < n_grp:
                s_next = scores(h + 1)
            if mask is not None:
                s = jnp.where(mask, s, -jnp.inf)
            m_prev = m_ref[rs, :]
            m_new = jnp.maximum(m_prev, jnp.max(s, axis=-1, keepdims=True))
            alpha = jnp.exp(m_prev - m_new)
            pr = jnp.exp(s - m_new)
            l_ref[rs, :] = alpha * l_ref[rs, :] + jnp.sum(pr, axis=-1, keepdims=True)
            m_ref[rs, :] = m_new
            pv = _dot(pr, kvb)
            if pending is not None:
                prs, palpha, ppv = pending
                acc_ref[prs, :] = palpha * acc_ref[prs, :] + ppv
            pending = (rs, alpha, pv)
        prs, palpha, ppv = pending
        acc_ref[prs, :] = palpha * acc_ref[prs, :] + ppv

    if n_past:
        def past_body(kb, carry):
            off = pl.multiple_of(kb * tk_past, tk_past)
            flash_step(ckv_ref[0, pl.ds(off, tk_past), :].astype(BF16),
                       ckr_ref[0, :, pl.ds(off, tk_past)].astype(BF16), None, kr_transposed=True)
            return carry
        lax.fori_loop(0, n_past // tk_past, past_body, 0)
        flash_step(kv_b, kr_b, None)
    else:
        off_j = pl.multiple_of(j * tq, tq)
        kvs_ref[pl.ds(off_j, tq), :] = kv_b
        krs_ref[pl.ds(off_j, tq), :] = kr_b

        def prev_body(kb, carry):
            off = pl.multiple_of(kb * tq, tq)
            flash_step(kvs_ref[pl.ds(off, tq), :], krs_ref[pl.ds(off, tq), :], None)
            return carry
        lax.fori_loop(0, j, prev_body, 0)
        q_chunk = (_iota((grp_rows, tq), 0) & (tq - 1)) >> 6
        k_chunk = _iota((grp_rows, tq), 1) >> 6
        flash_step(kv_b, kr_b, k_chunk <= q_chunk)

    o = acc_ref[...] / l_ref[...]
    for h in range(MLA_HEADS):
        mla_ref[0, :, h * MLA_VDIM:(h + 1) * MLA_VDIM] = _dot(o[h * tq:(h + 1) * tq, :], wuv_ref[h]).astype(BF16)


def _attn(pm, cos4, sin4, gq, gkv, wuq_p, wuk_t, wuv_t, cache_kv, cache_kr, tq):
    b, s, _ = pm.shape
    n_past = 0 if cache_kv is None else cache_kv.shape[1]
    rows = MLA_HEADS * tq
    const2 = lambda i, j: (0, 0)
    const3 = lambda i, j: (0, 0, 0)
    in_specs = [pl.BlockSpec((1, tq, MLA_SLOT), lambda i, j: (i, j, 0)),
                pl.BlockSpec((tq, LANES), lambda i, j: (j, 0)),
                pl.BlockSpec((tq, LANES), lambda i, j: (j, 0)),
                pl.BlockSpec((1, Q_LORA), const2),
                pl.BlockSpec((1, KV_LORA), const2),
                pl.BlockSpec(wuq_p.shape, const2),
                pl.BlockSpec(wuk_t.shape, const3),
                pl.BlockSpec(wuv_t.shape, const3)]
    args = [pm, cos4, sin4, gq, gkv, wuq_p, wuk_t, wuv_t]
    scratch = [pltpu.VMEM((rows, KV_LORA), BF16), pltpu.VMEM((rows, MLA_ROPE), BF16),
               pltpu.VMEM((rows, 1), F32), pltpu.VMEM((rows, 1), F32), pltpu.VMEM((rows, KV_LORA), F32)]
    if n_past:
        in_specs += [pl.BlockSpec((1, n_past, KV_LORA), lambda i, j: (i, 0, 0)),
                     pl.BlockSpec((1, MLA_ROPE, n_past), lambda i, j: (i, 0, 0))]
        args += [cache_kv, cache_kr]
    else:
        scratch += [pltpu.VMEM((s, KV_LORA), BF16), pltpu.VMEM((s, MLA_ROPE), BF16)]
    return pl.pallas_call(
        functools.partial(_attn_kernel, tq=tq, n_past=n_past, tk_past=min(ATT_TK_PAST, max(n_past, 1))),
        grid=(b, s // tq),
        in_specs=in_specs,
        out_specs=[pl.BlockSpec((1, tq, MLA_WIDTH), lambda i, j: (i, j, 0)),
                   pl.BlockSpec((1, tq, KV_LORA), lambda i, j: (i, j, 0)),
                   pl.BlockSpec((1, tq, MLA_ROPE), lambda i, j: (i, j, 0))],
        out_shape=[jax.ShapeDtypeStruct((b, s, MLA_WIDTH), BF16),
                   jax.ShapeDtypeStruct((b, s, KV_LORA), F32),
                   jax.ShapeDtypeStruct((b, s, MLA_ROPE), F32)],
        scratch_shapes=scratch,
        compiler_params=_params(("arbitrary", "arbitrary")),
    )(*args)


def _rwkv_kernel(pr_ref, shift_ref, zt0_ref, mu_ref, vec_ref, w2_ref, a2_ref, g2_ref,
                 out_ref, ztn_ref,
                 z_ref, carry_ref, y_ref, kt_ref, rt_ref, kh_ref, bh_ref, kb_ref, bb_ref, v_ref, wc_ref, *, tt):
    j = pl.program_id(1)
    nj = pl.num_programs(1)
    c = RWKV_WIDTH
    n_pairs = RWKV_HEADS // 2
    n_chunks = tt // CHUNK

    bd_mask = (_iota((PAIR, PAIR), 0) >> 6) == (_iota((PAIR, PAIR), 1) >> 6)

    @pl.when(j == 0)
    def _():
        carry_ref[...] = jnp.broadcast_to(shift_ref[0], carry_ref.shape)
        for pr in range(n_pairs):
            z_ref[pr] = jnp.zeros((PAIR, PAIR), F32)
            z_ref[pr, 0:RWKV_HEAD, 0:RWKV_HEAD] = zt0_ref[0, 2 * pr]
            z_ref[pr, RWKV_HEAD:PAIR, RWKV_HEAD:PAIR] = zt0_ref[0, 2 * pr + 1]

    rw = pr_ref[0]
    row = _iota((tt, 1), 0)
    prev = jnp.where(row == 0, carry_ref[0:1, :], pltpu.roll(rw, 1, axis=0))
    carry_ref[...] = jnp.broadcast_to(rw[tt - 1:tt, :], carry_ref.shape)
    xm = rw + (prev - rw) * mu_ref[...]

    w0 = vec_ref[0:1, :]
    a0 = vec_ref[1:2, :]
    k_k = vec_ref[2:3, :]
    k_a = vec_ref[3:4, :]
    r_k = vec_ref[4:5, :]
    lnx_g = vec_ref[5:6, :]
    lnx_b = vec_ref[6:7, :]

    r = xm[:, 0:c]
    k = xm[:, c:2 * c]
    v = xm[:, 2 * c:3 * c]
    lora = xm[:, 3 * c:3 * c + RW_LORA_SLOT]
    gd = xm[:, 3 * c + RW_LORA_SLOT:RW_SLOT]
    lw = -DECAY_SCALE * jax.nn.sigmoid(w0 + _dot(jnp.tanh(lora), w2_ref[...]))
    a = jax.nn.sigmoid(a0 + _dot(lora, a2_ref[...]))
    g = _dot(jax.nn.sigmoid(gd), g2_ref[...])

    ones_bd = jnp.where(bd_mask, 1.0, 0.0).astype(BF16)

    def head_sum(x):
        h, m, _ = _split3(x)
        cols = []
        for blk in range(c // LANES):
            sl = slice(blk * LANES, (blk + 1) * LANES)
            cols.append(jnp.dot(h[:, sl], ones_bd, preferred_element_type=F32)
                        + jnp.dot(m[:, sl], ones_bd, preferred_element_type=F32))
        return jnp.concatenate(cols, axis=1)

    kk = k * k_k
    kk = kk / jnp.maximum(jnp.sqrt(head_sum(kk * kk)), 1e-12)
    k_mod = k * (1.0 + (a - 1.0) * k_a)
    bonus = head_sum(r * k_mod * r_k) * v
    b_vec = kk * a

    tri = ((_iota((tt, tt), 0) >> 6) == (_iota((tt, tt), 1) >> 6)) & (_iota((tt, tt), 1) <= _iota((tt, tt), 0))
    tri_b = jnp.where(tri, 1.0, 0.0).astype(BF16)
    lw_h, lw_m, lw_l = _split3(lw)
    cum = (jnp.dot(tri_b, lw_h, preferred_element_type=F32) + jnp.dot(tri_b, lw_m, preferred_element_type=F32)
           + jnp.dot(tri_b, lw_l, preferred_element_type=F32))
    cum_end = jnp.concatenate(
        [jnp.broadcast_to(cum[(ci + 1) * CHUNK - 1:(ci + 1) * CHUNK, :], (CHUNK, c)) for ci in range(n_chunks)],
        axis=0)
    for ci in range(n_chunks):
        wc_ref[ci * SUBLANES:(ci + 1) * SUBLANES, :] = jnp.broadcast_to(
            jnp.exp(cum[(ci + 1) * CHUNK - 1:(ci + 1) * CHUNK, :]), (SUBLANES, c))
    e_inv = jnp.exp(-cum)
    e_end = jnp.exp(cum_end - cum)
    kt_ref[...] = (kk * jnp.exp(cum - lw)).astype(BF16)
    rt_ref[...] = (r * jnp.exp(cum)).astype(BF16)
    kh_ref[...] = (k_mod * e_inv).astype(BF16)
    bh_ref[...] = (b_vec * e_inv).astype(BF16)
    kb_ref[...] = (k_mod * e_end).astype(BF16)
    bb_ref[...] = (b_vec * e_end).astype(BF16)
    v_ref[...] = v.astype(BF16)

    t_i = _iota((CHUNK, PAIR), 0)
    j_i = _iota((CHUNK, PAIR), 1) & (RWKV_HEAD - 1)
    strict2 = j_i < t_i
    incl2 = j_i <= t_i
    blk16 = (j_i >> 4) == (t_i >> 4)
    eye2 = jnp.where(j_i == t_i, 1.0, 0.0)
    zero_b = jnp.zeros((PAIR, PAIR), BF16)

    def stack2(x):
        xb = x.astype(BF16)
        return jnp.where(bd_mask, jnp.concatenate([xb, xb], axis=0), zero_b)

    def pmm(x, y):
        return jnp.dot(x.astype(BF16), stack2(y), preferred_element_type=F32)

    items = [(ci, pr) for ci in range(n_chunks) for pr in range(n_pairs)]
    n_it = len(items)
    rsl = lambda ci: slice(ci * CHUNK, (ci + 1) * CHUNK)
    csl = lambda pr: slice(pr * PAIR, (pr + 1) * PAIR)
    each = lambda fn: [fn(i) for i in range(n_it)]

    kt = [kt_ref[rsl(ci), csl(pr)] for ci, pr in items]
    rt = [rt_ref[rsl(ci), csl(pr)] for ci, pr in items]
    vv = [v_ref[rsl(ci), csl(pr)] for ci, pr in items]
    khbh = [jnp.concatenate([stack2(kh_ref[rsl(ci), csl(pr)]), stack2(bh_ref[rsl(ci), csl(pr)])], axis=0)
            for ci, pr in items]
    a4 = each(lambda i: _dot_nt(jnp.concatenate([kt[i], rt[i]], axis=0), khbh[i]))
    m_mat = each(lambda i: jnp.where(strict2, a4[i][:CHUNK, :PAIR], 0.0))
    n_mat = each(lambda i: jnp.where(strict2, a4[i][:CHUNK, PAIR:], 0.0))
    p_mat = each(lambda i: jnp.where(incl2, a4[i][CHUNK:, :PAIR], 0.0))
    q_mat = each(lambda i: jnp.where(incl2, a4[i][CHUNK:, PAIR:], 0.0))
    n_d = each(lambda i: jnp.where(blk16, n_mat[i], 0.0))
    n_o = each(lambda i: n_mat[i] - n_d[i])
    n2 = each(lambda i: pmm(n_d[i], n_d[i]))
    t1 = each(lambda i: pmm(eye2 - n_d[i], eye2 + n2[i]))
    n4 = each(lambda i: pmm(n2[i], n2[i]))
    t2 = each(lambda i: pmm(t1[i], eye2 + n4[i]))
    n8 = each(lambda i: pmm(n4[i], n4[i]))
    t_d = each(lambda i: pmm(t2[i], eye2 + n8[i]))
    x1 = each(lambda i: pmm(t_d[i], n_o[i]))
    x2 = each(lambda i: pmm(x1[i], x1[i]))
    t3 = each(lambda i: pmm(eye2 - x1[i], eye2 + x2[i]))
    t_mat = each(lambda i: pmm(t3[i], t_d[i]))
    mvpv = each(lambda i: jnp.dot(jnp.concatenate([m_mat[i], p_mat[i]], axis=0).astype(BF16), stack2(vv[i]),
                                  preferred_element_type=F32))
    tkmv = each(lambda i: jnp.dot(t_mat[i].astype(BF16),
                                  jnp.concatenate([stack2(kt[i]), stack2(mvpv[i][:CHUNK])], axis=1),
                                  preferred_element_type=F32))
    z = [z_ref[pr] for pr in range(n_pairs)]
    for ci in range(n_chunks):
        ids = [ci * n_pairs + pr for pr in range(n_pairs)]
        zr = [jnp.dot(jnp.concatenate([tkmv[i][:, :PAIR].astype(BF16), rt[i]], axis=0), z[pr].astype(BF16),
                      preferred_element_type=F32) for pr, i in enumerate(ids)]
        sk = [zr[pr][:CHUNK] + tkmv[i][:, PAIR:] for pr, i in enumerate(ids)]
        qs = [pmm(q_mat[i], sk[pr]) for pr, i in enumerate(ids)]
        upd = [_dot_tn(jnp.concatenate([kb_ref[rsl(ci), csl(pr)], bb_ref[rsl(ci), csl(pr)]], axis=0),
                       jnp.concatenate([vv[i], (-sk[pr]).astype(BF16)], axis=0)) for pr, i in enumerate(ids)]
        for pr, i in enumerate(ids):
            y_ref[rsl(ci), csl(pr)] = zr[pr][CHUNK:] + mvpv[i][CHUNK:] - qs[pr]
            wc_col = jnp.transpose(wc_ref[ci * SUBLANES:(ci + 1) * SUBLANES, csl(pr)])[:, 0:1]
            z[pr] = z[pr] * wc_col + jnp.where(bd_mask, upd[pr], 0.0)
    for pr in range(n_pairs):
        z_ref[pr] = z[pr]


    y = y_ref[...]
    mean = head_sum(y) * (1.0 / RWKV_HEAD)
    dlt = y - mean
    var = head_sum(dlt * dlt) * (1.0 / RWKV_HEAD)
    yn = dlt * lax.rsqrt(var + GN_EPS) * lnx_g + lnx_b
    out_ref[0] = ((yn + bonus) * g).astype(BF16)

    @pl.when(j == nj - 1)
    def _():
        for pr in range(n_pairs):
            ztn_ref[0, 2 * pr] = z_ref[pr, 0:RWKV_HEAD, 0:RWKV_HEAD]
            ztn_ref[0, 2 * pr + 1] = z_ref[pr, RWKV_HEAD:PAIR, RWKV_HEAD:PAIR]


def _rwkv(pr, shift_p, zt0, mu_p, vecs, w2_p, a2_p, g2_p, tt):
    b, s, _ = pr.shape
    c = RWKV_WIDTH
    const2 = lambda i, j: (0, 0)
    full = lambda shape: pl.BlockSpec(shape, const2)
    return pl.pallas_call(
        functools.partial(_rwkv_kernel, tt=tt),
        grid=(b, s // tt),
        in_specs=[pl.BlockSpec((1, tt, RW_SLOT), lambda i, j: (i, j, 0)),
                  pl.BlockSpec((1, 1, RW_SLOT), lambda i, j: (i, 0, 0)),
                  pl.BlockSpec((1, RWKV_HEADS, RWKV_HEAD, RWKV_HEAD), lambda i, j: (i, 0, 0, 0)),
                  full((1, RW_SLOT)), full((SUBLANES, c)),
                  full(w2_p.shape), full(a2_p.shape), full(g2_p.shape)],
        out_specs=[pl.BlockSpec((1, tt, c), lambda i, j: (i, j, 0)),
                   pl.BlockSpec((1, RWKV_HEADS, RWKV_HEAD, RWKV_HEAD), lambda i, j: (i, 0, 0, 0))],
        out_shape=[jax.ShapeDtypeStruct((b, s, c), BF16),
                   jax.ShapeDtypeStruct((b, RWKV_HEADS, RWKV_HEAD, RWKV_HEAD), F32)],
        scratch_shapes=[pltpu.VMEM((RWKV_HEADS // 2, PAIR, PAIR), F32),
                        pltpu.VMEM((SUBLANES, RW_SLOT), F32),
                        pltpu.VMEM((tt, c), F32)]
                       + [pltpu.VMEM((tt, c), BF16) for _ in range(7)]
                       + [pltpu.VMEM((SUBLANES * (tt // CHUNK), c), F32)],
        compiler_params=_params(("arbitrary", "arbitrary")),
    )(pr, shift_p, zt0, mu_p, vecs, w2_p, a2_p, g2_p)


def _layer_norm(h, g, b):
    mu = jnp.mean(h, axis=-1, keepdims=True)
    d = h - mu
    var = jnp.mean(d * d, axis=-1, keepdims=True)
    return d * lax.rsqrt(var + LN_EPS) * g + b


def _outproj_kernel(*refs, n_own):
    if n_own is None:
        _outproj_tile(*refs)
        return
    tail_ref = refs[9]
    t = pl.program_id(0)

    @pl.when(t < n_own)
    def _():
        _outproj_tile(*refs[:9], *refs[10:])

    @pl.when(t >= n_own)
    def _():
        refs[11][...] = tail_ref[...]


def _outproj_tile(mla_ref, rwk_ref, x_ref, mod_ref, wo1_ref, wo2_ref, ln_ref, wr_ref, rb_ref,
                  x1_ref, u2_ref, route_ref, cnt_ref):
    gb, ts, d = x_ref.shape
    rows = gb * ts
    mix = (jnp.dot(mla_ref[...].reshape(rows, MLA_WIDTH), wo1_ref[...], preferred_element_type=F32)
           + jnp.dot(rwk_ref[...].reshape(rows, RWKV_WIDTH), wo2_ref[...], preferred_element_type=F32))
    h = DEEPNORM_ALPHA * x_ref[...] + mod_ref[:, 2:3, :] * mix.reshape(gb, ts, d)
    x1 = _layer_norm(h, ln_ref[0:1, :], ln_ref[1:2, :])
    u2 = (x1 * (1.0 + mod_ref[:, 4:5, :]) + mod_ref[:, 3:4, :]).reshape(rows, d)
    x1_ref[...] = x1.reshape(rows, d)
    u2_ref[...] = u2

    uh, um, _ = _split3(u2)
    lg = (jnp.dot(uh, wr_ref[0], preferred_element_type=F32) + jnp.dot(um, wr_ref[0], preferred_element_type=F32)
          + jnp.dot(uh, wr_ref[1], preferred_element_type=F32)) + rb_ref[...]
    lane = _iota((rows, LANES), 1).astype(F32)
    neg = -jnp.inf
    big = float(LANES)
    gl = jnp.where(lane < N_GROUPS, lg, neg)
    gmax = jnp.max(gl, axis=-1, keepdims=True)
    p_grp = 1.0 / jnp.sum(jnp.exp(gl - gmax), axis=-1, keepdims=True)
    grp = jnp.min(jnp.where(gl == gmax, lane, big), axis=-1, keepdims=True)
    lo = N_GROUPS + grp * EXPERTS_PER_GROUP
    el = jnp.where((lane >= lo) & (lane < lo + EXPERTS_PER_GROUP), lg, neg)
    v1 = jnp.max(el, axis=-1, keepdims=True)
    i1 = jnp.min(jnp.where(el == v1, lane, big), axis=-1, keepdims=True)
    el2 = jnp.where(lane == i1, neg, el)
    v2 = jnp.max(el2, axis=-1, keepdims=True)
    i2 = jnp.min(jnp.where(el2 == v2, lane, big), axis=-1, keepdims=True)
    e2 = jnp.exp(v2 - v1)
    den = 1.0 / (1.0 + e2)
    route_ref[...] = jnp.where(lane == 0, i1 - N_GROUPS,
                               jnp.where(lane == 1, i2 - N_GROUPS,
                                         jnp.where(lane == 2, p_grp * den,
                                                   jnp.where(lane == 3, p_grp * e2 * den, 0.0))))
    hits = jnp.where((lane == i1 - N_GROUPS) | (lane == i2 - N_GROUPS), 1.0, 0.0)
    cnt_ref[0] = jnp.broadcast_to(jnp.sum(hits, axis=0, keepdims=True), (SUBLANES, LANES))


def _outproj(mla, rwk, x, mod, wo1, wo2, ln, wr3, rb, gb, ts, u2_tail=None):
    b, s, d = x.shape
    rows = gb * ts
    nj = s // ts
    n_own = (b // gb) * nj
    n_tail = 0 if u2_tail is None else u2_tail.shape[0] // rows
    own = lambda t: jnp.minimum(t, n_own - 1)
    seq3 = lambda t: (own(t) // nj, own(t) % nj, 0)
    const2 = lambda t: (0, 0)
    once = pl.Buffered(1)
    in_specs = [pl.BlockSpec((gb, ts, MLA_WIDTH), seq3),
                pl.BlockSpec((gb, ts, RWKV_WIDTH), seq3),
                pl.BlockSpec((gb, ts, d), seq3),
                pl.BlockSpec((gb, 6, d), lambda t: (own(t) // nj, 0, 0)),
                pl.BlockSpec(wo1.shape, const2, pipeline_mode=once),
                pl.BlockSpec(wo2.shape, const2, pipeline_mode=once),
                pl.BlockSpec(ln.shape, const2),
                pl.BlockSpec(wr3.shape, lambda t: (0, 0, 0), pipeline_mode=once),
                pl.BlockSpec(rb.shape, const2)]
    args = [mla, rwk, x, mod, wo1, wo2, ln, wr3, rb]
    if n_tail:
        in_specs.append(pl.BlockSpec((rows, d), lambda t: (jnp.maximum(t - n_own, 0), 0)))
        args.append(u2_tail)
    return pl.pallas_call(
        functools.partial(_outproj_kernel, n_own=n_own if n_tail else None),
        grid=(n_own + n_tail,),
        in_specs=in_specs,
        out_specs=[pl.BlockSpec((rows, d), lambda t: (own(t), 0)),
                   pl.BlockSpec((rows, d), lambda t: (t, 0)),
                   pl.BlockSpec((rows, LANES), lambda t: (own(t), 0)),
                   pl.BlockSpec((1, SUBLANES, LANES), lambda t: (own(t), 0, 0))],
        out_shape=[jax.ShapeDtypeStruct((b * s, d), F32),
                   jax.ShapeDtypeStruct(((n_own + n_tail) * rows, d), F32),
                   jax.ShapeDtypeStruct((b * s, LANES), F32),
                   jax.ShapeDtypeStruct((n_own, SUBLANES, LANES), F32)],
        compiler_params=_params(("arbitrary",)),
    )(*args)


def _moe_kernel(bexp_ref, code_ref, nval_ref, nused_ref,
                u2_hbm, wg_ref, wu_ref, wd_ref,
                y_hbm,
                xbuf, ybuf, wgb, wub, wdb, gsem, ssem, *, n_tok):
    i = pl.program_id(0)
    nused = nused_ref[0]
    slot = i % 3

    def gather_copy(src_row, xs, r):
        return pltpu.make_async_copy(u2_hbm.at[pl.ds(src_row, 1)], xbuf.at[xs, pl.ds(r, 1)], gsem.at[xs])

    def scatter_copy(blk, r, for_wait=False):
        if for_wait:
            dst = 0
        else:
            code = code_ref[blk * MOE_BLOCK + r]
            dst = ((code >> 1) & 1) * n_tok + (code >> 2)
        return pltpu.make_async_copy(ybuf.at[blk % 3, pl.ds(r, 1)], y_hbm.at[pl.ds(dst, 1)], ssem.at[blk % 3])

    def issue_gather(blk, xs):
        for r in range(MOE_BLOCK):
            gather_copy(code_ref[blk * MOE_BLOCK + r] >> 2, xs, r).start()

    def wait_gather(xs):
        def body(r, carry):
            gather_copy(0, xs, r).wait()
            return carry
        lax.fori_loop(0, MOE_BLOCK, body, 0, unroll=8)

    def scatter_partial(blk, start):
        def body(r, carry):
            if start:
                scatter_copy(blk, r).start()
            else:
                scatter_copy(blk, r, for_wait=True).wait()
            return carry
        lax.fori_loop(0, nval_ref[blk], body, 0)

    def wait_scatter(blk):
        @pl.when(nval_ref[blk] == MOE_BLOCK)
        def _():
            def body(r, carry):
                scatter_copy(blk, r, for_wait=True).wait()
                return carry
            lax.fori_loop(0, MOE_BLOCK, body, 0, unroll=8)

        @pl.when(nval_ref[blk] < MOE_BLOCK)
        def _():
            scatter_partial(blk, False)

    def compute(with_prev_scatter):
        issue_gather(jnp.minimum(i + 2, nused - 1), (i + 2) % 3)
        if with_prev_scatter:
            for r in range(MOE_BLOCK):
                scatter_copy(i - 1, r).start()
        xb = xbuf[slot].astype(BF16)
        hg = jnp.dot(xb, wgb[...], preferred_element_type=F32)
        hu = jnp.dot(xb, wub[...], preferred_element_type=F32)
        hh = (hg * jax.nn.sigmoid(hg) * hu).astype(BF16)
        ybuf[i % 3] = jnp.dot(hh, wdb[...], preferred_element_type=F32)

    @pl.when(i == 0)
    def _():
        issue_gather(0, 0)
        issue_gather(jnp.minimum(1, nused - 1), 1)

    @pl.when(i <= nused)
    def _():
        wait_gather(slot)

        @pl.when(i >= 3)
        def _():
            wait_scatter(i - 3)

    @pl.when(i < nused)
    def _():
        prev_e = bexp_ref[jnp.maximum(i - 1, 0)]

        @pl.when((i == 0) | (bexp_ref[i] != prev_e))
        def _():
            wgb[...] = wg_ref[0].astype(BF16)
            wub[...] = wu_ref[0].astype(BF16)
            wdb[...] = wd_ref[0].astype(BF16)

        prev_full = (i >= 1) & (nval_ref[jnp.maximum(i - 1, 0)] == MOE_BLOCK)

        @pl.when(prev_full)
        def _():
            compute(True)

        @pl.when(jnp.logical_not(prev_full))
        def _():
            compute(False)

            @pl.when(i >= 1)
            def _():
                scatter_partial(i - 1, True)

    @pl.when(i == nused)
    def _():
        wait_gather((i + 1) % 3)
        scatter_partial(i - 1, True)

        @pl.when(i >= 2)
        def _():
            wait_scatter(i - 2)
        wait_scatter(i - 1)


def _moe(u2_all, code, n_valid, block_expert, n_used, wg, wu, wd):
    n_tok, d = u2_all.shape
    n_blocks = block_expert.shape[0]
    de = wg.shape[2]

    def wmap(i, bexp, code_r, nval, nused):
        return (bexp[jnp.minimum(i, nused[0] - 1)], 0, 0)

    grid_spec = pltpu.PrefetchScalarGridSpec(
        num_scalar_prefetch=4,
        grid=(n_blocks + 1,),
        in_specs=[pl.BlockSpec(memory_space=pl.ANY),
                  pl.BlockSpec((1, d, de), wmap),
                  pl.BlockSpec((1, d, de), wmap),
                  pl.BlockSpec((1, de, d), wmap)],
        out_specs=pl.BlockSpec(memory_space=pl.ANY),
        scratch_shapes=[pltpu.VMEM((3, MOE_BLOCK, d), F32), pltpu.VMEM((3, MOE_BLOCK, d), F32),
                        pltpu.VMEM((d, de), BF16), pltpu.VMEM((d, de), BF16), pltpu.VMEM((de, d), BF16),
                        pltpu.SemaphoreType.DMA((3,)), pltpu.SemaphoreType.DMA((3,))])
    return pl.pallas_call(
        functools.partial(_moe_kernel, n_tok=n_tok),
        grid_spec=grid_spec,
        out_shape=jax.ShapeDtypeStruct((TOP_K * n_tok, d), F32),
        compiler_params=_params(("arbitrary",)),
    )(block_expert, code, n_valid, n_used, u2_all, wg, wu, wd)


def _rank_kernel(route_ref, base_ref, o_ref):
    rows = route_ref.shape[0]
    lane = _iota((rows, LANES), 1).astype(F32)
    hit1 = lane == route_ref[:, 0:1]
    hit2 = lane == route_ref[:, 1:2]
    hits = jnp.where(hit1 | hit2, 1.0, 0.0).astype(BF16)
    earlier = jnp.where(_iota((rows, rows), 1) < _iota((rows, rows), 0), 1.0, 0.0).astype(BF16)
    pos = jnp.dot(earlier, hits, preferred_element_type=F32) + base_ref[0, 0:1, :]
    d1 = jnp.sum(jnp.where(hit1, pos, 0.0), axis=-1, keepdims=True)
    d2 = jnp.sum(jnp.where(hit2, pos, 0.0), axis=-1, keepdims=True)
    o_ref[...] = jnp.where(lane == 0, d1, jnp.where(lane == 1, d2, 0.0))


def _route_tables(route, tile_counts):
    n_tok = route.shape[0]
    n_asg = n_tok * TOP_K
    n_tiles = tile_counts.shape[0]
    rows = n_tok // n_tiles
    tc = tile_counts[:, 0, :].astype(jnp.int32)
    counts = jnp.sum(tc, axis=0)
    padded = (counts + MOE_BLOCK - 1) // MOE_BLOCK * MOE_BLOCK
    pad_ends = jnp.cumsum(padded)
    pad_starts = pad_ends - padded
    base = (pad_starts[None, :] + jnp.cumsum(tc, axis=0) - tc).astype(F32)
    base = jnp.broadcast_to(base[:, None, :], (n_tiles, SUBLANES, LANES))
    dest = pl.pallas_call(
        _rank_kernel,
        grid=(n_tiles,),
        in_specs=[pl.BlockSpec((rows, LANES), lambda i: (i, 0)),
                  pl.BlockSpec((1, SUBLANES, LANES), lambda i: (i, 0, 0))],
        out_specs=pl.BlockSpec((rows, LANES), lambda i: (i, 0)),
        out_shape=jax.ShapeDtypeStruct((n_tok, LANES), F32),
        compiler_params=_params(("arbitrary",)),
    )(route, base)
    dest = dest[:, 0:TOP_K].astype(jnp.int32).reshape(n_asg)
    n_blocks = -(-(n_asg + N_EXPERTS * (MOE_BLOCK - 1)) // MOE_BLOCK)
    n_rows = n_blocks * MOE_BLOCK
    code = jnp.zeros((n_rows,), jnp.int32).at[dest].set(
        jnp.arange(n_asg, dtype=jnp.int32) * 2 + 1, unique_indices=True)
    blk_start = jnp.arange(n_blocks, dtype=jnp.int32) * MOE_BLOCK
    block_expert = jnp.minimum(
        jnp.sum((pad_ends[None, :N_EXPERTS] <= blk_start[:, None]).astype(jnp.int32), axis=1),
        N_EXPERTS - 1).astype(jnp.int32)
    n_used = (pad_ends[N_EXPERTS - 1:N_EXPERTS] // MOE_BLOCK).astype(jnp.int32)
    filled = (pad_starts + counts)[:N_EXPERTS]
    n_valid = jnp.where(blk_start < pad_ends[N_EXPERTS - 1],
                        jnp.clip(filled[block_expert] - blk_start, 0, MOE_BLOCK), 0).astype(jnp.int32)
    return code, n_valid, block_expert, n_used


def _final_kernel(x1_ref, y0_ref, y1_ref, route_ref, mod_ref, ln_ref, o_ref):
    gb, ts, d = o_ref.shape
    moe = (route_ref[:, 2:3] * y0_ref[...] + route_ref[:, 3:4] * y1_ref[...]).reshape(gb, ts, d)
    h = DEEPNORM_ALPHA * x1_ref[...].reshape(gb, ts, d) + mod_ref[:, 5:6, :] * moe
    o_ref[...] = _layer_norm(h, ln_ref[0:1, :], ln_ref[1:2, :])


def _final(x1, y, route, mod, ln, b, s, gb, ts, tile0, n_tok):
    d = x1.shape[1]
    rows = gb * ts
    nj = s // ts
    k_off = n_tok // rows
    return pl.pallas_call(
        _final_kernel,
        grid=(b // gb, nj),
        in_specs=[pl.BlockSpec((rows, d), lambda i, j: (i * nj + j, 0)),
                  pl.BlockSpec((rows, d), lambda i, j: (tile0 + i * nj + j, 0)),
                  pl.BlockSpec((rows, d), lambda i, j: (k_off + tile0 + i * nj + j, 0)),
                  pl.BlockSpec((rows, LANES), lambda i, j: (i * nj + j, 0)),
                  pl.BlockSpec((gb, 6, d), lambda i, j: (i, 0, 0)),
                  pl.BlockSpec(ln.shape, lambda i, j: (0, 0))],
        out_specs=pl.BlockSpec((gb, ts, d), lambda i, j: (i, j, 0)),
        out_shape=jax.ShapeDtypeStruct((b, s, d), F32),
        compiler_params=_params(("arbitrary", "arbitrary")),
    )(x1, y, y, route, mod, ln)


def _rope_tables(n_past, s):
    inv_freq = ROPE_THETA ** (-jnp.arange(0, MLA_ROPE, 2, dtype=F32) / MLA_ROPE)
    ang = jnp.arange(n_past, n_past + s).astype(F32)[:, None] * inv_freq[None, :]
    cos, sin = jnp.cos(ang), jnp.sin(ang)
    return jnp.concatenate([cos, cos, cos, cos], axis=1), jnp.concatenate([-sin, sin, -sin, sin], axis=1)


def _pad_cols(w, n):
    return jnp.pad(w, ((0, 0), (0, n - w.shape[1])))


def kernel(x_prompt, x_sample, c_prompt, c_sample, cache_kv_latent, cache_k_rope, state_shift, state_wkv, w_ada, b_ada, w_in, q_norm_g, w_uq, kv_norm_g, w_uk, w_uv, rwkv_mu, rwkv_w0, rwkv_w2, rwkv_a0, rwkv_a2, rwkv_g2, rwkv_k_k, rwkv_k_a, rwkv_r_k, rwkv_lnx_g, rwkv_lnx_b, w_out, ln1_g, ln1_b, router_group_w, router_group_b, router_expert_w, router_expert_b, expert_w_gate, expert_w_up, expert_w_down, ln2_g, ln2_b):
    depth = w_in.shape[0]
    assert depth == 1
    bp, sp, d = x_prompt.shape
    bs, ss, _ = x_sample.shape
    n_past = cache_kv_latent.shape[2]
    assert d == D_MODEL and sp % ROW_TILE == 0 and ss == CHUNK and bs % (ROW_TILE // CHUNK) == 0
    assert sp % ATT_TQ == 0 and n_past % ATT_TK_PAST == 0
    gb_s = ROW_TILE // ss
    mla_proj = Q_LORA + KV_LORA + MLA_ROPE

    wi = w_in[0]
    w_in_p = jnp.concatenate(
        [wi[:, :mla_proj], wi[:, mla_proj - MLA_ROPE:mla_proj], _pad_cols(wi[:, mla_proj:], RW_SLOT)],
        axis=1).astype(BF16)
    wq = w_uq[0].reshape(Q_LORA, MLA_HEADS, MLA_QK)
    wq_rope = wq[:, :, MLA_NOPE:]
    wuq_p = jnp.concatenate(
        [wq[:, :, :MLA_NOPE].reshape(Q_LORA, MLA_WIDTH),
         jnp.concatenate([wq_rope, wq_rope], axis=2).reshape(Q_LORA, MLA_HEADS * LANES)], axis=1).astype(BF16)
    wuk_t = jnp.transpose(w_uk[0], (1, 2, 0)).astype(BF16)
    wuv_t = jnp.transpose(w_uv[0], (1, 0, 2)).astype(BF16)
    gq = q_norm_g[0].reshape(1, Q_LORA)
    gkv = kv_norm_g[0].reshape(1, KV_LORA)
    mu_p = _pad_cols(rwkv_mu[0].reshape(1, RWKV_PROJ), RW_SLOT)
    zeros_c = jnp.zeros((RWKV_WIDTH,), F32)
    vecs = jnp.stack([rwkv_w0[0], rwkv_a0[0], rwkv_k_k[0], rwkv_k_a[0], rwkv_r_k[0].reshape(RWKV_WIDTH),
                      rwkv_lnx_g[0], rwkv_lnx_b[0], zeros_c])
    w2_p = jnp.concatenate([rwkv_w2[0], jnp.zeros((AAA_LORA, RWKV_WIDTH), F32)], axis=0).astype(BF16)
    a2_p = jnp.concatenate([jnp.zeros((DECAY_LORA, RWKV_WIDTH), F32), rwkv_a2[0]], axis=0).astype(BF16)
    g2_p = jnp.concatenate([rwkv_g2[0], jnp.zeros((RW_GATE_SLOT - GATE_LORA, RWKV_WIDTH), F32)],
                           axis=0).astype(BF16)
    wo1 = w_out[0][:MLA_WIDTH].astype(BF16)
    wo2 = w_out[0][MLA_WIDTH:].astype(BF16)
    ln1 = jnp.stack([ln1_g[0], ln1_b[0]])
    ln2 = jnp.stack([ln2_g[0], ln2_b[0]])
    wr = _pad_cols(jnp.concatenate([router_group_w[0], router_expert_w[0]], axis=1), LANES)
    wr3 = jnp.stack(_split3(wr))
    rb = _pad_cols(jnp.concatenate([router_group_b[0], router_expert_b[0]]).reshape(1, -1), LANES)

    mod = _ada(jnp.concatenate([c_prompt, c_sample], axis=0), w_ada[0], b_ada[0]).reshape(bp + bs, 6, d)
    mod_p, mod_s = mod[:bp], mod[bp:]

    def mix_group(x, mod_g, gb, ts, tq, tt, cache_kv, cache_kr, shift_prev, wkv_prev, u2_tail=None):
        b, s, _ = x.shape
        pm, prw = _inproj(x, mod_g, w_in_p, gb, ts)
        n_p = 0 if cache_kv is None else cache_kv.shape[1]
        cos4, sin4 = _rope_tables(n_p, s)
        mla, kv_new, kr_new = _attn(pm, cos4, sin4, gq, gkv, wuq_p, wuk_t, wuv_t, cache_kv, cache_kr, tq)
        shift_p = _pad_cols(shift_prev.reshape(b, RWKV_PROJ), RW_SLOT).reshape(b, 1, RW_SLOT)
        zt0 = jnp.swapaxes(wkv_prev.astype(F32), -1, -2)
        rwk, ztn = _rwkv(prw, shift_p, zt0, mu_p, vecs, w2_p, a2_p, g2_p, tt)
        x1, u2, route, cnt = _outproj(mla, rwk, x, mod_g, wo1, wo2, ln1, wr3, rb, gb, ts, u2_tail)
        shift_new = prw[:, s - 1:s, :RWKV_PROJ]
        return x1, u2, (route, cnt), kv_new, kr_new, shift_new, jnp.swapaxes(ztn, -1, -2)

    zero_shift = jnp.zeros((bp, 1, RWKV_PROJ), F32)
    zero_wkv = jnp.zeros((bp, RWKV_HEADS, RWKV_HEAD, RWKV_HEAD), F32)
    x1_s, u2_s, route_s, kv_s, kr_s, sh_s, wkv_s = mix_group(
        x_sample, mod_s, gb_s, ss, ss, ss, cache_kv_latent[0], jnp.swapaxes(cache_k_rope[0], 1, 2),
        state_shift[0], state_wkv[0])
    x1_p, u2_all, route_p, kv_p, kr_p, sh_p, wkv_p = mix_group(
        x_prompt, mod_p, 1, ROW_TILE, ATT_TQ, RWKV_TT, None, None, zero_shift, zero_wkv, u2_s)

    n_p_tok = bp * sp
    n_tok = n_p_tok + bs * ss
    code, n_valid, block_expert, n_used = _route_tables(jnp.concatenate([route_p[0], route_s[0]], axis=0),
                                                        jnp.concatenate([route_p[1], route_s[1]], axis=0))
    y = _moe(u2_all, code, n_valid, block_expert, n_used,
             expert_w_gate[0], expert_w_up[0], expert_w_down[0])

    out_p = _final(x1_p, y, route_p[0], mod_p, ln2, bp, sp, 1, ROW_TILE, 0, n_tok)
    out_s = _final(x1_s, y, route_s[0], mod_s, ln2, bs, ss, gb_s, ss, n_p_tok // ROW_TILE, n_tok)
    return (out_p, out_s, kv_p[None], kr_p[None], sh_p[None], wkv_p[None],
            kv_s[None], kr_s[None], sh_s[None], wkv_s[None])
```

```python
import functools

import jax
import jax.numpy as jnp
from jax import lax
from jax.experimental import pallas as pl
from jax.experimental.pallas import tpu as pltpu

F32 = jnp.float32
BF16 = jnp.bfloat16

LANES = 128
SUBLANES = 8

D_MODEL = 2048
CHUNK = 64
MLA_HEADS = 8
MLA_NOPE = 128
MLA_ROPE = 64
MLA_VDIM = 128
MLA_QK = MLA_NOPE + MLA_ROPE
MLA_WIDTH = MLA_HEADS * MLA_VDIM
Q_LORA = 512
KV_LORA = 512
RWKV_HEADS = 16
RWKV_HEAD = 64
RWKV_WIDTH = RWKV_HEADS * RWKV_HEAD
DECAY_LORA = 64
AAA_LORA = 64
GATE_LORA = 160
RWKV_PROJ = 3 * RWKV_WIDTH + DECAY_LORA + AAA_LORA + GATE_LORA
N_GROUPS = 8
EXPERTS_PER_GROUP = 8
N_EXPERTS = N_GROUPS * EXPERTS_PER_GROUP
TOP_K = 2
D_EXPERT = 512
MOE_BLOCK = 128
ROPE_THETA = 10000.0
LN_EPS = 1e-5
RMS_EPS = 1e-6
GN_EPS = 64e-5
DECAY_SCALE = 0.606531
SOFTMAX_SCALE = MLA_QK ** -0.5
DEEPNORM_ALPHA = 2.0 ** 0.25

MLA_SLOT = Q_LORA + KV_LORA + LANES
RW_LORA_SLOT = DECAY_LORA + AAA_LORA
RW_GATE_SLOT = 2 * LANES
RW_SLOT = 3 * RWKV_WIDTH + RW_LORA_SLOT + RW_GATE_SLOT
ROW_TILE = 512
ATT_TQ = 256
ATT_TK_PAST = 256
RWKV_TT = 128
PAIR = 2 * RWKV_HEAD
VMEM_LIMIT = 56 * 1024 * 1024


def _dot(a, b):
    return jnp.dot(a.astype(BF16), b.astype(BF16), preferred_element_type=F32)


def _dot_nt(a, b):
    return lax.dot_general(a.astype(BF16), b.astype(BF16), (((1,), (1,)), ((), ())),
                           preferred_element_type=F32)


def _dot_tn(a, b):
    return lax.dot_general(a.astype(BF16), b.astype(BF16), (((0,), (0,)), ((), ())),
                           preferred_element_type=F32)


def _split3(x):
    h = x.astype(BF16)
    r1 = x - h.astype(F32)
    m = r1.astype(BF16)
    l = (r1 - m.astype(F32)).astype(BF16)
    return h, m, l


def _iota(shape, dim):
    return lax.broadcasted_iota(jnp.int32, shape, dim)


def _params(sem):
    return pltpu.CompilerParams(dimension_semantics=sem, vmem_limit_bytes=VMEM_LIMIT)


def _ada_kernel(c_ref, w_ref, b_ref, o_ref):
    c = c_ref[...]
    s = c * jax.nn.sigmoid(c)
    o_ref[...] = _dot(s, w_ref[...]) + b_ref[...]


def _ada(c_all, w_ada, b_ada):
    nb, d = c_all.shape
    n = w_ada.shape[1]
    tn = 1536
    return pl.pallas_call(
        _ada_kernel,
        grid=(n // tn,),
        in_specs=[pl.BlockSpec((nb, d), lambda j: (0, 0)),
                  pl.BlockSpec((d, tn), lambda j: (0, j)),
                  pl.BlockSpec((1, tn), lambda j: (0, j))],
        out_specs=pl.BlockSpec((nb, tn), lambda j: (0, j)),
        out_shape=jax.ShapeDtypeStruct((nb, n), F32),
        compiler_params=_params(("arbitrary",)),
    )(c_all, w_ada, b_ada.reshape(1, n))


def _inproj_kernel(x_ref, mod_ref, w_ref, om_ref, or_ref, u_ref):
    n = pl.program_id(2)
    gb, ts, d = x_ref.shape

    @pl.when(n == 0)
    def _():
        u = x_ref[...] * (1.0 + mod_ref[:, 1:2, :]) + mod_ref[:, 0:1, :]
        u_ref[...] = u.reshape(gb * ts, d).astype(BF16)

    res = jnp.dot(u_ref[...], w_ref[...], preferred_element_type=F32)

    @pl.when(n == 0)
    def _():
        om_ref[...] = res.reshape(om_ref.shape)

    @pl.when(n > 0)
    def _():
        or_ref[...] = res.reshape(or_ref.shape)


def _inproj(x, mod, w_packed, gb, ts):
    b, s, d = x.shape
    tn = MLA_SLOT
    nt = w_packed.shape[1] // tn
    return pl.pallas_call(
        _inproj_kernel,
        grid=(b // gb, s // ts, nt),
        in_specs=[pl.BlockSpec((gb, ts, d), lambda i, j, n: (i, j, 0)),
                  pl.BlockSpec((gb, 6, d), lambda i, j, n: (i, 0, 0)),
                  pl.BlockSpec((d, tn), lambda i, j, n: (0, n))],
        out_specs=[pl.BlockSpec((gb, ts, tn), lambda i, j, n: (i, j, 0)),
                   pl.BlockSpec((gb, ts, tn), lambda i, j, n: (i, j, jnp.maximum(n - 1, 0)))],
        out_shape=[jax.ShapeDtypeStruct((b, s, MLA_SLOT), F32),
                   jax.ShapeDtypeStruct((b, s, RW_SLOT), F32)],
        scratch_shapes=[pltpu.VMEM((gb * ts, d), BF16)],
        compiler_params=_params(("arbitrary", "arbitrary", "arbitrary")),
    )(x, mod, w_packed)


def _rms(x, g):
    return x * lax.rsqrt(jnp.mean(jnp.square(x), axis=-1, keepdims=True) + RMS_EPS) * g


def _rope_slot(slot, cos4, sin4):
    return slot * cos4 + pltpu.roll(slot, MLA_ROPE // 2, axis=1) * sin4


def _attn_kernel(*refs, tq, n_past, tk_past):
    if n_past:
        (pm_ref, cos_ref, sin_ref, gq_ref, gkv_ref, wuq_ref, wuk_ref, wuv_ref, ckv_ref, ckr_ref,
         mla_ref, kv_ref, kr_ref, qlat_ref, qrope_ref, m_ref, l_ref, acc_ref) = refs
    else:
        (pm_ref, cos_ref, sin_ref, gq_ref, gkv_ref, wuq_ref, wuk_ref, wuv_ref,
         mla_ref, kv_ref, kr_ref, qlat_ref, qrope_ref, m_ref, l_ref, acc_ref, kvs_ref, krs_ref) = refs
    j = pl.program_id(1)
    rows = MLA_HEADS * tq
    p = pm_ref[0]
    cos4 = cos_ref[...]
    sin4 = sin_ref[...]

    kv = _rms(p[:, Q_LORA:Q_LORA + KV_LORA], gkv_ref[...])
    kr = _rope_slot(p[:, Q_LORA + KV_LORA:MLA_SLOT], cos4, sin4)[:, :MLA_ROPE]
    kv_ref[0] = kv
    kr_ref[0] = kr
    kv_b = kv.astype(BF16)
    kr_b = kr.astype(BF16)

    q = _dot(_rms(p[:, :Q_LORA], gq_ref[...]), wuq_ref[...])
    for h in range(MLA_HEADS):
        qlat = _dot(q[:, h * MLA_NOPE:(h + 1) * MLA_NOPE], wuk_ref[h])
        rot = _rope_slot(q[:, MLA_WIDTH + h * LANES:MLA_WIDTH + (h + 1) * LANES], cos4, sin4)
        qlat_ref[h * tq:(h + 1) * tq, :] = (qlat * SOFTMAX_SCALE).astype(BF16)
        qrope_ref[h * tq:(h + 1) * tq, :] = (rot[:, :MLA_ROPE] * SOFTMAX_SCALE).astype(BF16)

    m_ref[...] = jnp.full(m_ref.shape, -jnp.inf, F32)
    l_ref[...] = jnp.zeros(l_ref.shape, F32)
    acc_ref[...] = jnp.zeros(acc_ref.shape, F32)

    grp_rows = max(tq, ATT_TQ)
    n_grp = rows // grp_rows

    def flash_step(kvb, krb, mask, kr_transposed=False):
        def scores(h):
            rs = slice(h * grp_rows, (h + 1) * grp_rows)
            rope = _dot(qrope_ref[rs, :], krb) if kr_transposed else _dot_nt(qrope_ref[rs, :], krb)
            return _dot_nt(qlat_ref[rs, :], kvb) + rope

        pending = None
        s_next = scores(0)
        for h in range(n_grp):
            rs = slice(h * grp_rows, (h + 1) * grp_rows)
            s = s_next
            if h + 1 < n_grp:
                s_next = scores(h + 1)
            if mask is not None:
                s = jnp.where(mask, s, -jnp.inf)
            m_prev = m_ref[rs, :]
            m_new = jnp.maximum(m_prev, jnp.max(s, axis=-1, keepdims=True))
            alpha = jnp.exp(m_prev - m_new)
            pr = jnp.exp(s - m_new)
            l_ref[rs, :] = alpha * l_ref[rs, :] + jnp.sum(pr, axis=-1, keepdims=True)
            m_ref[rs, :] = m_new
            pv = _dot(pr, kvb)
            if pending is not None:
                prs, palpha, ppv = pending
                acc_ref[prs, :] = palpha * acc_ref[prs, :] + ppv
            pending = (rs, alpha, pv)
        prs, palpha, ppv = pending
        acc_ref[prs, :] = palpha * acc_ref[prs, :] + ppv

    if n_past:
        def past_body(kb, carry):
            off = pl.multiple_of(kb * tk_past, tk_past)
            flash_step(ckv_ref[0, pl.ds(off, tk_past), :].astype(BF16),
                       ckr_ref[0, :, pl.ds(off, tk_past)].astype(BF16), None, kr_transposed=True)
            return carry
        lax.fori_loop(0, n_past // tk_past, past_body, 0)
        flash_step(kv_b, kr_b, None)
    else:
        off_j = pl.multiple_of(j * tq, tq)
        kvs_ref[pl.ds(off_j, tq), :] = kv_b
        krs_ref[pl.ds(off_j, tq), :] = kr_b

        def prev_body(kb, carry):
            off = pl.multiple_of(kb * tq, tq)
            flash_step(kvs_ref[pl.ds(off, tq), :], krs_ref[pl.ds(off, tq), :], None)
            return carry
        lax.fori_loop(0, j, prev_body, 0)
        q_chunk = (_iota((grp_rows, tq), 0) & (tq - 1)) >> 6
        k_chunk = _iota((grp_rows, tq), 1) >> 6
        flash_step(kv_b, kr_b, k_chunk <= q_chunk)

    o = acc_ref[...] / l_ref[...]
    for h in range(MLA_HEADS):
        mla_ref[0, :, h * MLA_VDIM:(h + 1) * MLA_VDIM] = _dot(o[h * tq:(h + 1) * tq, :], wuv_ref[h]).astype(BF16)


def _attn(pm, cos4, sin4, gq, gkv, wuq_p, wuk_t, wuv_t, cache_kv, cache_kr, tq):
    b, s, _ = pm.shape
    n_past = 0 if cache_kv is None else cache_kv.shape[1]
    rows = MLA_HEADS * tq
    const2 = lambda i, j: (0, 0)
    const3 = lambda i, j: (0, 0, 0)
    in_specs = [pl.BlockSpec((1, tq, MLA_SLOT), lambda i, j: (i, j, 0)),
                pl.BlockSpec((tq, LANES), lambda i, j: (j, 0)),
                pl.BlockSpec((tq, LANES), lambda i, j: (j, 0)),
                pl.BlockSpec((1, Q_LORA), const2),
                pl.BlockSpec((1, KV_LORA), const2),
                pl.BlockSpec(wuq_p.shape, const2),
                pl.BlockSpec(wuk_t.shape, const3),
                pl.BlockSpec(wuv_t.shape, const3)]
    args = [pm, cos4, sin4, gq, gkv, wuq_p, wuk_t, wuv_t]
    scratch = [pltpu.VMEM((rows, KV_LORA), BF16), pltpu.VMEM((rows, MLA_ROPE), BF16),
               pltpu.VMEM((rows, 1), F32), pltpu.VMEM((rows, 1), F32), pltpu.VMEM((rows, KV_LORA), F32)]
    if n_past:
        in_specs += [pl.BlockSpec((1, n_past, KV_LORA), lambda i, j: (i, 0, 0)),
                     pl.BlockSpec((1, MLA_ROPE, n_past), lambda i, j: (i, 0, 0))]
        args += [cache_kv, cache_kr]
    else:
        scratch += [pltpu.VMEM((s, KV_LORA), BF16), pltpu.VMEM((s, MLA_ROPE), BF16)]
    return pl.pallas_call(
        functools.partial(_attn_kernel, tq=tq, n_past=n_past, tk_past=min(ATT_TK_PAST, max(n_past, 1))),
        grid=(b, s // tq),
        in_specs=in_specs,
        out_specs=[pl.BlockSpec((1, tq, MLA_WIDTH), lambda i, j: (i, j, 0)),
                   pl.BlockSpec((1, tq, KV_LORA), lambda i, j: (i, j, 0)),
                   pl.BlockSpec((1, tq, MLA_ROPE), lambda i, j: (i, j, 0))],
        out_shape=[jax.ShapeDtypeStruct((b, s, MLA_WIDTH), BF16),
                   jax.ShapeDtypeStruct((b, s, KV_LORA), F32),
                   jax.ShapeDtypeStruct((b, s, MLA_ROPE), F32)],
        scratch_shapes=scratch,
        compiler_params=_params(("arbitrary", "arbitrary")),
    )(*args)


def _rwkv_kernel(pr_ref, shift_ref, zt0_ref, mu_ref, vec_ref, w2_ref, a2_ref, g2_ref,
                 out_ref, ztn_ref,
                 z_ref, carry_ref, y_ref, kt_ref, rt_ref, kh_ref, bh_ref, kb_ref, bb_ref, v_ref, wc_ref, *, tt):
    j = pl.program_id(1)
    nj = pl.num_programs(1)
    c = RWKV_WIDTH
    n_pairs = RWKV_HEADS // 2
    n_chunks = tt // CHUNK

    bd_mask = (_iota((PAIR, PAIR), 0) >> 6) == (_iota((PAIR, PAIR), 1) >> 6)

    @pl.when(j == 0)
    def _():
        carry_ref[...] = jnp.broadcast_to(shift_ref[0], carry_ref.shape)
        for pr in range(n_pairs):
            z_ref[pr] = jnp.zeros((PAIR, PAIR), F32)
            z_ref[pr, 0:RWKV_HEAD, 0:RWKV_HEAD] = zt0_ref[0, 2 * pr]
            z_ref[pr, RWKV_HEAD:PAIR, RWKV_HEAD:PAIR] = zt0_ref[0, 2 * pr + 1]

    rw = pr_ref[0]
    row = _iota((tt, 1), 0)
    prev = jnp.where(row == 0, carry_ref[0:1, :], pltpu.roll(rw, 1, axis=0))
    carry_ref[...] = jnp.broadcast_to(rw[tt - 1:tt, :], carry_ref.shape)
    xm = rw + (prev - rw) * mu_ref[...]

    w0 = vec_ref[0:1, :]
    a0 = vec_ref[1:2, :]
    k_k = vec_ref[2:3, :]
    k_a = vec_ref[3:4, :]
    r_k = vec_ref[4:5, :]
    lnx_g = vec_ref[5:6, :]
    lnx_b = vec_ref[6:7, :]

    r = xm[:, 0:c]
    k = xm[:, c:2 * c]
    v = xm[:, 2 * c:3 * c]
    lora = xm[:, 3 * c:3 * c + RW_LORA_SLOT]
    gd = xm[:, 3 * c + RW_LORA_SLOT:RW_SLOT]
    lw = -DECAY_SCALE * jax.nn.sigmoid(w0 + _dot(jnp.tanh(lora), w2_ref[...]))
    a = jax.nn.sigmoid(a0 + _dot(lora, a2_ref[...]))
    g = _dot(jax.nn.sigmoid(gd), g2_ref[...])

    ones_bd = jnp.where(bd_mask, 1.0, 0.0).astype(BF16)

    def head_sum(x):
        h, m, _ = _split3(x)
        cols = []
        for blk in range(c // LANES):
            sl = slice(blk * LANES, (blk + 1) * LANES)
            cols.append(jnp.dot(h[:, sl], ones_bd, preferred_element_type=F32)
                        + jnp.dot(m[:, sl], ones_bd, preferred_element_type=F32))
        return jnp.concatenate(cols, axis=1)

    kk = k * k_k
    kk = kk / jnp.maximum(jnp.sqrt(head_sum(kk * kk)), 1e-12)
    k_mod = k * (1.0 + (a - 1.0) * k_a)
    bonus = head_sum(r * k_mod * r_k) * v
    b_vec = kk * a

    tri = ((_iota((tt, tt), 0) >> 6) == (_iota((tt, tt), 1) >> 6)) & (_iota((tt, tt), 1) <= _iota((tt, tt), 0))
    tri_b = jnp.where(tri, 1.0, 0.0).astype(BF16)
    lw_h, lw_m, lw_l = _split3(lw)
    cum = (jnp.dot(tri_b, lw_h, preferred_element_type=F32) + jnp.dot(tri_b, lw_m, preferred_element_type=F32)
           + jnp.dot(tri_b, lw_l, preferred_element_type=F32))
    cum_end = jnp.concatenate(
        [jnp.broadcast_to(cum[(ci + 1) * CHUNK - 1:(ci + 1) * CHUNK, :], (CHUNK, c)) for ci in range(n_chunks)],
        axis=0)
    for ci in range(n_chunks):
        wc_ref[ci * SUBLANES:(ci + 1) * SUBLANES, :] = jnp.broadcast_to(
            jnp.exp(cum[(ci + 1) * CHUNK - 1:(ci + 1) * CHUNK, :]), (SUBLANES, c))
    e_inv = jnp.exp(-cum)
    e_end = jnp.exp(cum_end - cum)
    kt_ref[...] = (kk * jnp.exp(cum - lw)).astype(BF16)
    rt_ref[...] = (r * jnp.exp(cum)).astype(BF16)
    kh_ref[...] = (k_mod * e_inv).astype(BF16)
    bh_ref[...] = (b_vec * e_inv).astype(BF16)
    kb_ref[...] = (k_mod * e_end).astype(BF16)
    bb_ref[...] = (b_vec * e_end).astype(BF16)
    v_ref[...] = v.astype(BF16)

    t_i = _iota((CHUNK, PAIR), 0)
    j_i = _iota((CHUNK, PAIR), 1) & (RWKV_HEAD - 1)
    strict2 = j_i < t_i
    incl2 = j_i <= t_i
    blk16 = (j_i >> 4) == (t_i >> 4)
    eye2 = jnp.where(j_i == t_i, 1.0, 0.0)
    zero_b = jnp.zeros((PAIR, PAIR), BF16)

    def stack2(x):
        xb = x.astype(BF16)
        return jnp.where(bd_mask, jnp.concatenate([xb, xb], axis=0), zero_b)

    def pmm(x, y):
        return jnp.dot(x.astype(BF16), stack2(y), preferred_element_type=F32)

    items = [(ci, pr) for ci in range(n_chunks) for pr in range(n_pairs)]
    n_it = len(items)
    rsl = lambda ci: slice(ci * CHUNK, (ci + 1) * CHUNK)
    csl = lambda pr: slice(pr * PAIR, (pr + 1) * PAIR)
    each = lambda fn: [fn(i) for i in range(n_it)]

    kt = [kt_ref[rsl(ci), csl(pr)] for ci, pr in items]
    rt = [rt_ref[rsl(ci), csl(pr)] for ci, pr in items]
    vv = [v_ref[rsl(ci), csl(pr)] for ci, pr in items]
    khbh = [jnp.concatenate([stack2(kh_ref[rsl(ci), csl(pr)]), stack2(bh_ref[rsl(ci), csl(pr)])], axis=0)
            for ci, pr in items]
    a4 = each(lambda i: _dot_nt(jnp.concatenate([kt[i], rt[i]], axis=0), khbh[i]))
    m_mat = each(lambda i: jnp.where(strict2, a4[i][:CHUNK, :PAIR], 0.0))
    n_mat = each(lambda i: jnp.where(strict2, a4[i][:CHUNK, PAIR:], 0.0))
    p_mat = each(lambda i: jnp.where(incl2, a4[i][CHUNK:, :PAIR], 0.0))
    q_mat = each(lambda i: jnp.where(incl2, a4[i][CHUNK:, PAIR:], 0.0))
    n_d = each(lambda i: jnp.where(blk16, n_mat[i], 0.0))
    n_o = each(lambda i: n_mat[i] - n_d[i])
    n2 = each(lambda i: pmm(n_d[i], n_d[i]))
    t1 = each(lambda i: pmm(eye2 - n_d[i], eye2 + n2[i]))
    n4 = each(lambda i: pmm(n2[i], n2[i]))
    t2 = each(lambda i: pmm(t1[i], eye2 + n4[i]))
    n8 = each(lambda i: pmm(n4[i], n4[i]))
    t_d = each(lambda i: pmm(t2[i], eye2 + n8[i]))
    x1 = each(lambda i: pmm(t_d[i], n_o[i]))
    x2 = each(lambda i: pmm(x1[i], x1[i]))
    t3 = each(lambda i: pmm(eye2 - x1[i], eye2 + x2[i]))
    t_mat = each(lambda i: pmm(t3[i], t_d[i]))
    mvpv = each(lambda i: jnp.dot(jnp.concatenate([m_mat[i], p_mat[i]], axis=0).astype(BF16), stack2(vv[i]),
                                  preferred_element_type=F32))
    tkmv = each(lambda i: jnp.dot(t_mat[i].astype(BF16),
                                  jnp.concatenate([stack2(kt[i]), stack2(mvpv[i][:CHUNK])], axis=1),
                                  preferred_element_type=F32))
    z = [z_ref[pr] for pr in range(n_pairs)]
    for ci in range(n_chunks):
        ids = [ci * n_pairs + pr for pr in range(n_pairs)]
        zr = [jnp.dot(jnp.concatenate([tkmv[i][:, :PAIR].astype(BF16), rt[i]], axis=0), z[pr].astype(BF16),
                      preferred_element_type=F32) for pr, i in enumerate(ids)]
        sk = [zr[pr][:CHUNK] + tkmv[i][:, PAIR:] for pr, i in enumerate(ids)]
        qs = [pmm(q_mat[i], sk[pr]) for pr, i in enumerate(ids)]
        upd = [_dot_tn(jnp.concatenate([kb_ref[rsl(ci), csl(pr)], bb_ref[rsl(ci), csl(pr)]], axis=0),
                       jnp.concatenate([vv[i], (-sk[pr]).astype(BF16)], axis=0)) for pr, i in enumerate(ids)]
        for pr, i in enumerate(ids):
            y_ref[rsl(ci), csl(pr)] = zr[pr][CHUNK:] + mvpv[i][CHUNK:] - qs[pr]
            wc_col = jnp.transpose(wc_ref[ci * SUBLANES:(ci + 1) * SUBLANES, csl(pr)])[:, 0:1]
            z[pr] = z[pr] * wc_col + jnp.where(bd_mask, upd[pr], 0.0)
    for pr in range(n_pairs):
        z_ref[pr] = z[pr]


    y = y_ref[...]
    mean = head_sum(y) * (1.0 / RWKV_HEAD)
    dlt = y - mean
    var = head_sum(dlt * dlt) * (1.0 / RWKV_HEAD)
    yn = dlt * lax.rsqrt(var + GN_EPS) * lnx_g + lnx_b
    out_ref[0] = ((yn + bonus) * g).astype(BF16)

    @pl.when(j == nj - 1)
    def _():
        for pr in range(n_pairs):
            ztn_ref[0, 2 * pr] = z_ref[pr, 0:RWKV_HEAD, 0:RWKV_HEAD]
            ztn_ref[0, 2 * pr + 1] = z_ref[pr, RWKV_HEAD:PAIR, RWKV_HEAD:PAIR]


def _rwkv(pr, shift_p, zt0, mu_p, vecs, w2_p, a2_p, g2_p, tt):
    b, s, _ = pr.shape
    c = RWKV_WIDTH
    const2 = lambda i, j: (0, 0)
    full = lambda shape: pl.BlockSpec(shape, const2)
    return pl.pallas_call(
        functools.partial(_rwkv_kernel, tt=tt),
        grid=(b, s // tt),
        in_specs=[pl.BlockSpec((1, tt, RW_SLOT), lambda i, j: (i, j, 0)),
                  pl.BlockSpec((1, 1, RW_SLOT), lambda i, j: (i, 0, 0)),
                  pl.BlockSpec((1, RWKV_HEADS, RWKV_HEAD, RWKV_HEAD), lambda i, j: (i, 0, 0, 0)),
                  full((1, RW_SLOT)), full((SUBLANES, c)),
                  full(w2_p.shape), full(a2_p.shape), full(g2_p.shape)],
        out_specs=[pl.BlockSpec((1, tt, c), lambda i, j: (i, j, 0)),
                   pl.BlockSpec((1, RWKV_HEADS, RWKV_HEAD, RWKV_HEAD), lambda i, j: (i, 0, 0, 0))],
        out_shape=[jax.ShapeDtypeStruct((b, s, c), BF16),
                   jax.ShapeDtypeStruct((b, RWKV_HEADS, RWKV_HEAD, RWKV_HEAD), F32)],
        scratch_shapes=[pltpu.VMEM((RWKV_HEADS // 2, PAIR, PAIR), F32),
                        pltpu.VMEM((SUBLANES, RW_SLOT), F32),
                        pltpu.VMEM((tt, c), F32)]
                       + [pltpu.VMEM((tt, c), BF16) for _ in range(7)]
                       + [pltpu.VMEM((SUBLANES * (tt // CHUNK), c), F32)],
        compiler_params=_params(("arbitrary", "arbitrary")),
    )(pr, shift_p, zt0, mu_p, vecs, w2_p, a2_p, g2_p)


def _layer_norm(h, g, b):
    mu = jnp.mean(h, axis=-1, keepdims=True)
    d = h - mu
    var = jnp.mean(d * d, axis=-1, keepdims=True)
    return d * lax.rsqrt(var + LN_EPS) * g + b


def _store_token_major(ref, x):
    n = x.shape[0]
    for c in range(D_MODEL // LANES):
        ref[pl.ds(c, n, stride=D_MODEL // LANES), :] = x[:, c * LANES:(c + 1) * LANES]


def _load_token_major(ref, n, lead=()):
    cols = [ref[lead + (pl.ds(c, n, stride=D_MODEL // LANES), slice(None))] for c in range(D_MODEL // LANES)]
    return jnp.concatenate(cols, axis=1)


def _outproj_kernel(*refs, n_own):
    if n_own is None:
        _outproj_tile(*refs)
        return
    tail_ref = refs[9]
    t = pl.program_id(0)

    @pl.when(t < n_own)
    def _():
        _outproj_tile(*refs[:9], *refs[10:])

    @pl.when(t >= n_own)
    def _():
        refs[11][...] = tail_ref[...]


def _outproj_tile(mla_ref, rwk_ref, x_ref, mod_ref, wo1_ref, wo2_ref, ln_ref, wr_ref, rb_ref,
                  x1_ref, u2_ref, route_ref, cnt_ref):
    gb, ts, d = x_ref.shape
    rows = gb * ts
    mix = (jnp.dot(mla_ref[...].reshape(rows, MLA_WIDTH), wo1_ref[...], preferred_element_type=F32)
           + jnp.dot(rwk_ref[...].reshape(rows, RWKV_WIDTH), wo2_ref[...], preferred_element_type=F32))
    h = DEEPNORM_ALPHA * x_ref[...] + mod_ref[:, 2:3, :] * mix.reshape(gb, ts, d)
    x1 = _layer_norm(h, ln_ref[0:1, :], ln_ref[1:2, :])
    u2 = (x1 * (1.0 + mod_ref[:, 4:5, :]) + mod_ref[:, 3:4, :]).reshape(rows, d)
    x1_ref[...] = x1.reshape(rows, d)
    _store_token_major(u2_ref, u2)

    uh, um, _ = _split3(u2)
    lg = (jnp.dot(uh, wr_ref[0], preferred_element_type=F32) + jnp.dot(um, wr_ref[0], preferred_element_type=F32)
          + jnp.dot(uh, wr_ref[1], preferred_element_type=F32)) + rb_ref[...]
    lane = _iota((rows, LANES), 1).astype(F32)
    neg = -jnp.inf
    big = float(LANES)
    gl = jnp.where(lane < N_GROUPS, lg, neg)
    gmax = jnp.max(gl, axis=-1, keepdims=True)
    p_grp = 1.0 / jnp.sum(jnp.exp(gl - gmax), axis=-1, keepdims=True)
    grp = jnp.min(jnp.where(gl == gmax, lane, big), axis=-1, keepdims=True)
    lo = N_GROUPS + grp * EXPERTS_PER_GROUP
    el = jnp.where((lane >= lo) & (lane < lo + EXPERTS_PER_GROUP), lg, neg)
    v1 = jnp.max(el, axis=-1, keepdims=True)
    i1 = jnp.min(jnp.where(el == v1, lane, big), axis=-1, keepdims=True)
    el2 = jnp.where(lane == i1, neg, el)
    v2 = jnp.max(el2, axis=-1, keepdims=True)
    i2 = jnp.min(jnp.where(el2 == v2, lane, big), axis=-1, keepdims=True)
    e2 = jnp.exp(v2 - v1)
    den = 1.0 / (1.0 + e2)
    route_ref[...] = jnp.where(lane == 0, i1 - N_GROUPS,
                               jnp.where(lane == 1, i2 - N_GROUPS,
                                         jnp.where(lane == 2, p_grp * den,
                                                   jnp.where(lane == 3, p_grp * e2 * den, 0.0))))
    hits = jnp.where((lane == i1 - N_GROUPS) | (lane == i2 - N_GROUPS), 1.0, 0.0)
    cnt_ref[0] = jnp.broadcast_to(jnp.sum(hits, axis=0, keepdims=True), (SUBLANES, LANES))


def _outproj(mla, rwk, x, mod, wo1, wo2, ln, wr3, rb, gb, ts, u2_tail=None):
    b, s, d = x.shape
    rows = gb * ts
    nj = s // ts
    n_own = (b // gb) * nj
    lg = d // LANES
    n_tail = 0 if u2_tail is None else u2_tail.shape[0] // (rows * lg)
    own = lambda t: jnp.minimum(t, n_own - 1)
    seq3 = lambda t: (own(t) // nj, own(t) % nj, 0)
    const2 = lambda t: (0, 0)
    once = pl.Buffered(1)
    in_specs = [pl.BlockSpec((gb, ts, MLA_WIDTH), seq3),
                pl.BlockSpec((gb, ts, RWKV_WIDTH), seq3),
                pl.BlockSpec((gb, ts, d), seq3),
                pl.BlockSpec((gb, 6, d), lambda t: (own(t) // nj, 0, 0)),
                pl.BlockSpec(wo1.shape, const2, pipeline_mode=once),
                pl.BlockSpec(wo2.shape, const2, pipeline_mode=once),
                pl.BlockSpec(ln.shape, const2),
                pl.BlockSpec(wr3.shape, lambda t: (0, 0, 0), pipeline_mode=once),
                pl.BlockSpec(rb.shape, const2)]
    args = [mla, rwk, x, mod, wo1, wo2, ln, wr3, rb]
    if n_tail:
        in_specs.append(pl.BlockSpec((rows * lg, LANES), lambda t: (jnp.maximum(t - n_own, 0), 0)))
        args.append(u2_tail)
    return pl.pallas_call(
        functools.partial(_outproj_kernel, n_own=n_own if n_tail else None),
        grid=(n_own + n_tail,),
        in_specs=in_specs,
        out_specs=[pl.BlockSpec((rows, d), lambda t: (own(t), 0)),
                   pl.BlockSpec((rows * lg, LANES), lambda t: (t, 0)),
                   pl.BlockSpec((rows, LANES), lambda t: (own(t), 0)),
                   pl.BlockSpec((1, SUBLANES, LANES), lambda t: (own(t), 0, 0))],
        out_shape=[jax.ShapeDtypeStruct((b * s, d), F32),
                   jax.ShapeDtypeStruct(((n_own + n_tail) * rows * lg, LANES), F32),
                   jax.ShapeDtypeStruct((b * s, LANES), F32),
                   jax.ShapeDtypeStruct((n_own, SUBLANES, LANES), F32)],
        compiler_params=_params(("arbitrary",)),
    )(*args)


def _moe_kernel(bexp_ref, code_ref, nval_ref, nused_ref,
                u2_hbm, wg_ref, wu_ref, wd_ref,
                y_hbm,
                xbuf, ybuf, wgb, wub, wdb, gsem, ssem, *, n_tok):
    i = pl.program_id(0)
    nused = nused_ref[0]
    slot = i % 3

    lg = D_MODEL // LANES

    def gather_copy(src_tok, xs, r):
        return pltpu.make_async_copy(u2_hbm.at[pl.ds(pl.multiple_of(src_tok * lg, lg), lg)],
                                     xbuf.at[xs, pl.ds(r * lg, lg)], gsem.at[xs])

    def scatter_copy(blk, r, for_wait=False):
        if for_wait:
            dst = 0
        else:
            code = code_ref[blk * MOE_BLOCK + r]
            dst = ((code >> 1) & 1) * n_tok + (code >> 2)
        return pltpu.make_async_copy(ybuf.at[blk % 3, pl.ds(r * lg, lg)],
                                     y_hbm.at[pl.ds(pl.multiple_of(dst * lg, lg), lg)], ssem.at[blk % 3])

    def issue_gather(blk, xs):
        for r in range(MOE_BLOCK):
            gather_copy(code_ref[blk * MOE_BLOCK + r] >> 2, xs, r).start()

    def wait_gather(xs):
        def body(r, carry):
            gather_copy(0, xs, r).wait()
            return carry
        lax.fori_loop(0, MOE_BLOCK, body, 0, unroll=8)

    def scatter_partial(blk, start):
        def body(r, carry):
            if start:
                scatter_copy(blk, r).start()
            else:
                scatter_copy(blk, r, for_wait=True).wait()
            return carry
        lax.fori_loop(0, nval_ref[blk], body, 0)

    def wait_scatter(blk):
        @pl.when(nval_ref[blk] == MOE_BLOCK)
        def _():
            def body(r, carry):
                scatter_copy(blk, r, for_wait=True).wait()
                return carry
            lax.fori_loop(0, MOE_BLOCK, body, 0, unroll=8)

        @pl.when(nval_ref[blk] < MOE_BLOCK)
        def _():
            scatter_partial(blk, False)

    def compute(with_prev_scatter):
        issue_gather(jnp.minimum(i + 2, nused - 1), (i + 2) % 3)
        if with_prev_scatter:
            for r in range(MOE_BLOCK):
                scatter_copy(i - 1, r).start()
        xb = _load_token_major(xbuf, MOE_BLOCK, (slot,)).astype(BF16)
        hg = jnp.dot(xb, wgb[...], preferred_element_type=F32)
        hu = jnp.dot(xb, wub[...], preferred_element_type=F32)
        hh = (hg * jax.nn.sigmoid(hg) * hu).astype(BF16)
        _store_token_major(ybuf.at[i % 3], jnp.dot(hh, wdb[...], preferred_element_type=F32))

    @pl.when(i == 0)
    def _():
        issue_gather(0, 0)
        issue_gather(jnp.minimum(1, nused - 1), 1)

    @pl.when(i <= nused)
    def _():
        wait_gather(slot)

        @pl.when(i >= 3)
        def _():
            wait_scatter(i - 3)

    @pl.when(i < nused)
    def _():
        prev_e = bexp_ref[jnp.maximum(i - 1, 0)]

        @pl.when((i == 0) | (bexp_ref[i] != prev_e))
        def _():
            wgb[...] = wg_ref[0].astype(BF16)
            wub[...] = wu_ref[0].astype(BF16)
            wdb[...] = wd_ref[0].astype(BF16)

        prev_full = (i >= 1) & (nval_ref[jnp.maximum(i - 1, 0)] == MOE_BLOCK)

        @pl.when(prev_full)
        def _():
            compute(True)

        @pl.when(jnp.logical_not(prev_full))
        def _():
            compute(False)

            @pl.when(i >= 1)
            def _():
                scatter_partial(i - 1, True)

    @pl.when(i == nused)
    def _():
        wait_gather((i + 1) % 3)
        scatter_partial(i - 1, True)

        @pl.when(i >= 2)
        def _():
            wait_scatter(i - 2)
        wait_scatter(i - 1)


def _moe(u2_all, code, n_valid, block_expert, n_used, wg, wu, wd):
    d = wg.shape[1]
    lg = d // LANES
    n_tok = u2_all.shape[0] // lg
    n_blocks = block_expert.shape[0]
    de = wg.shape[2]

    def wmap(i, bexp, code_r, nval, nused):
        return (bexp[jnp.minimum(i, nused[0] - 1)], 0, 0)

    grid_spec = pltpu.PrefetchScalarGridSpec(
        num_scalar_prefetch=4,
        grid=(n_blocks + 1,),
        in_specs=[pl.BlockSpec(memory_space=pl.ANY),
                  pl.BlockSpec((1, d, de), wmap),
                  pl.BlockSpec((1, d, de), wmap),
                  pl.BlockSpec((1, de, d), wmap)],
        out_specs=pl.BlockSpec(memory_space=pl.ANY),
        scratch_shapes=[pltpu.VMEM((3, MOE_BLOCK * lg, LANES), F32), pltpu.VMEM((3, MOE_BLOCK * lg, LANES), F32),
                        pltpu.VMEM((d, de), BF16), pltpu.VMEM((d, de), BF16), pltpu.VMEM((de, d), BF16),
                        pltpu.SemaphoreType.DMA((3,)), pltpu.SemaphoreType.DMA((3,))])
    return pl.pallas_call(
        functools.partial(_moe_kernel, n_tok=n_tok),
        grid_spec=grid_spec,
        out_shape=jax.ShapeDtypeStruct((TOP_K * n_tok * lg, LANES), F32),
        compiler_params=_params(("arbitrary",)),
    )(block_expert, code, n_valid, n_used, u2_all, wg, wu, wd)


def _rank_kernel(route_ref, base_ref, o_ref):
    rows = route_ref.shape[0]
    lane = _iota((rows, LANES), 1).astype(F32)
    hit1 = lane == route_ref[:, 0:1]
    hit2 = lane == route_ref[:, 1:2]
    hits = jnp.where(hit1 | hit2, 1.0, 0.0).astype(BF16)
    earlier = jnp.where(_iota((rows, rows), 1) < _iota((rows, rows), 0), 1.0, 0.0).astype(BF16)
    pos = jnp.dot(earlier, hits, preferred_element_type=F32) + base_ref[0, 0:1, :]
    d1 = jnp.sum(jnp.where(hit1, pos, 0.0), axis=-1, keepdims=True)
    d2 = jnp.sum(jnp.where(hit2, pos, 0.0), axis=-1, keepdims=True)
    o_ref[...] = jnp.where(lane == 0, d1, jnp.where(lane == 1, d2, 0.0))


def _route_tables(route, tile_counts):
    n_tok = route.shape[0]
    n_asg = n_tok * TOP_K
    n_tiles = tile_counts.shape[0]
    rows = n_tok // n_tiles
    tc = tile_counts[:, 0, :].astype(jnp.int32)
    counts = jnp.sum(tc, axis=0)
    padded = (counts + MOE_BLOCK - 1) // MOE_BLOCK * MOE_BLOCK
    pad_ends = jnp.cumsum(padded)
    pad_starts = pad_ends - padded
    base = (pad_starts[None, :] + jnp.cumsum(tc, axis=0) - tc).astype(F32)
    base = jnp.broadcast_to(base[:, None, :], (n_tiles, SUBLANES, LANES))
    dest = pl.pallas_call(
        _rank_kernel,
        grid=(n_tiles,),
        in_specs=[pl.BlockSpec((rows, LANES), lambda i: (i, 0)),
                  pl.BlockSpec((1, SUBLANES, LANES), lambda i: (i, 0, 0))],
        out_specs=pl.BlockSpec((rows, LANES), lambda i: (i, 0)),
        out_shape=jax.ShapeDtypeStruct((n_tok, LANES), F32),
        compiler_params=_params(("arbitrary",)),
    )(route, base)
    dest = dest[:, 0:TOP_K].astype(jnp.int32).reshape(n_asg)
    n_blocks = -(-(n_asg + N_EXPERTS * (MOE_BLOCK - 1)) // MOE_BLOCK)
    n_rows = n_blocks * MOE_BLOCK
    code = jnp.zeros((n_rows,), jnp.int32).at[dest].set(
        jnp.arange(n_asg, dtype=jnp.int32) * 2 + 1, unique_indices=True)
    blk_start = jnp.arange(n_blocks, dtype=jnp.int32) * MOE_BLOCK
    block_expert = jnp.minimum(
        jnp.sum((pad_ends[None, :N_EXPERTS] <= blk_start[:, None]).astype(jnp.int32), axis=1),
        N_EXPERTS - 1).astype(jnp.int32)
    n_used = (pad_ends[N_EXPERTS - 1:N_EXPERTS] // MOE_BLOCK).astype(jnp.int32)
    filled = (pad_starts + counts)[:N_EXPERTS]
    n_valid = jnp.where(blk_start < pad_ends[N_EXPERTS - 1],
                        jnp.clip(filled[block_expert] - blk_start, 0, MOE_BLOCK), 0).astype(jnp.int32)
    return code, n_valid, block_expert, n_used


def _final_kernel(x1_ref, y0_ref, y1_ref, route_ref, mod_ref, ln_ref, o_ref):
    gb, ts, d = o_ref.shape
    y0 = _load_token_major(y0_ref, gb * ts)
    y1 = _load_token_major(y1_ref, gb * ts)
    moe = (route_ref[:, 2:3] * y0 + route_ref[:, 3:4] * y1).reshape(gb, ts, d)
    h = DEEPNORM_ALPHA * x1_ref[...].reshape(gb, ts, d) + mod_ref[:, 5:6, :] * moe
    o_ref[...] = _layer_norm(h, ln_ref[0:1, :], ln_ref[1:2, :])


def _final(x1, y, route, mod, ln, b, s, gb, ts, tile0, n_tok):
    d = x1.shape[1]
    rows = gb * ts
    nj = s // ts
    k_off = n_tok // rows
    return pl.pallas_call(
        _final_kernel,
        grid=(b // gb, nj),
        in_specs=[pl.BlockSpec((rows, d), lambda i, j: (i * nj + j, 0)),
                  pl.BlockSpec((rows * (d // LANES), LANES), lambda i, j: (tile0 + i * nj + j, 0)),
                  pl.BlockSpec((rows * (d // LANES), LANES), lambda i, j: (k_off + tile0 + i * nj + j, 0)),
                  pl.BlockSpec((rows, LANES), lambda i, j: (i * nj + j, 0)),
                  pl.BlockSpec((gb, 6, d), lambda i, j: (i, 0, 0)),
                  pl.BlockSpec(ln.shape, lambda i, j: (0, 0))],
        out_specs=pl.BlockSpec((gb, ts, d), lambda i, j: (i, j, 0)),
        out_shape=jax.ShapeDtypeStruct((b, s, d), F32),
        compiler_params=_params(("arbitrary", "arbitrary")),
    )(x1, y, y, route, mod, ln)


def _rope_tables(n_past, s):
    inv_freq = ROPE_THETA ** (-jnp.arange(0, MLA_ROPE, 2, dtype=F32) / MLA_ROPE)
    ang = jnp.arange(n_past, n_past + s).astype(F32)[:, None] * inv_freq[None, :]
    cos, sin = jnp.cos(ang), jnp.sin(ang)
    return jnp.concatenate([cos, cos, cos, cos], axis=1), jnp.concatenate([-sin, sin, -sin, sin], axis=1)


def _pad_cols(w, n):
    return jnp.pad(w, ((0, 0), (0, n - w.shape[1])))


def kernel(x_prompt, x_sample, c_prompt, c_sample, cache_kv_latent, cache_k_rope, state_shift, state_wkv, w_ada, b_ada, w_in, q_norm_g, w_uq, kv_norm_g, w_uk, w_uv, rwkv_mu, rwkv_w0, rwkv_w2, rwkv_a0, rwkv_a2, rwkv_g2, rwkv_k_k, rwkv_k_a, rwkv_r_k, rwkv_lnx_g, rwkv_lnx_b, w_out, ln1_g, ln1_b, router_group_w, router_group_b, router_expert_w, router_expert_b, expert_w_gate, expert_w_up, expert_w_down, ln2_g, ln2_b):
    depth = w_in.shape[0]
    assert depth == 1
    bp, sp, d = x_prompt.shape
    bs, ss, _ = x_sample.shape
    n_past = cache_kv_latent.shape[2]
    assert d == D_MODEL and sp % ROW_TILE == 0 and ss == CHUNK and bs % (ROW_TILE // CHUNK) == 0
    assert sp % ATT_TQ == 0 and n_past % ATT_TK_PAST == 0
    gb_s = ROW_TILE // ss
    mla_proj = Q_LORA + KV_LORA + MLA_ROPE

    wi = w_in[0]
    w_in_p = jnp.concatenate(
        [wi[:, :mla_proj], wi[:, mla_proj - MLA_ROPE:mla_proj], _pad_cols(wi[:, mla_proj:], RW_SLOT)],
        axis=1).astype(BF16)
    wq = w_uq[0].reshape(Q_LORA, MLA_HEADS, MLA_QK)
    wq_rope = wq[:, :, MLA_NOPE:]
    wuq_p = jnp.concatenate(
        [wq[:, :, :MLA_NOPE].reshape(Q_LORA, MLA_WIDTH),
         jnp.concatenate([wq_rope, wq_rope], axis=2).reshape(Q_LORA, MLA_HEADS * LANES)], axis=1).astype(BF16)
    wuk_t = jnp.transpose(w_uk[0], (1, 2, 0)).astype(BF16)
    wuv_t = jnp.transpose(w_uv[0], (1, 0, 2)).astype(BF16)
    gq = q_norm_g[0].reshape(1, Q_LORA)
    gkv = kv_norm_g[0].reshape(1, KV_LORA)
    mu_p = _pad_cols(rwkv_mu[0].reshape(1, RWKV_PROJ), RW_SLOT)
    zeros_c = jnp.zeros((RWKV_WIDTH,), F32)
    vecs = jnp.stack([rwkv_w0[0], rwkv_a0[0], rwkv_k_k[0], rwkv_k_a[0], rwkv_r_k[0].reshape(RWKV_WIDTH),
                      rwkv_lnx_g[0], rwkv_lnx_b[0], zeros_c])
    w2_p = jnp.concatenate([rwkv_w2[0], jnp.zeros((AAA_LORA, RWKV_WIDTH), F32)], axis=0).astype(BF16)
    a2_p = jnp.concatenate([jnp.zeros((DECAY_LORA, RWKV_WIDTH), F32), rwkv_a2[0]], axis=0).astype(BF16)
    g2_p = jnp.concatenate([rwkv_g2[0], jnp.zeros((RW_GATE_SLOT - GATE_LORA, RWKV_WIDTH), F32)],
                           axis=0).astype(BF16)
    wo1 = w_out[0][:MLA_WIDTH].astype(BF16)
    wo2 = w_out[0][MLA_WIDTH:].astype(BF16)
    ln1 = jnp.stack([ln1_g[0], ln1_b[0]])
    ln2 = jnp.stack([ln2_g[0], ln2_b[0]])
    wr = _pad_cols(jnp.concatenate([router_group_w[0], router_expert_w[0]], axis=1), LANES)
    wr3 = jnp.stack(_split3(wr))
    rb = _pad_cols(jnp.concatenate([router_group_b[0], router_expert_b[0]]).reshape(1, -1), LANES)

    mod = _ada(jnp.concatenate([c_prompt, c_sample], axis=0), w_ada[0], b_ada[0]).reshape(bp + bs, 6, d)
    mod_p, mod_s = mod[:bp], mod[bp:]

    def mix_group(x, mod_g, gb, ts, tq, tt, cache_kv, cache_kr, shift_prev, wkv_prev, u2_tail=None):
        b, s, _ = x.shape
        pm, prw = _inproj(x, mod_g, w_in_p, gb, ts)
        n_p = 0 if cache_kv is None else cache_kv.shape[1]
        cos4, sin4 = _rope_tables(n_p, s)
        mla, kv_new, kr_new = _attn(pm, cos4, sin4, gq, gkv, wuq_p, wuk_t, wuv_t, cache_kv, cache_kr, tq)
        shift_p = _pad_cols(shift_prev.reshape(b, RWKV_PROJ), RW_SLOT).reshape(b, 1, RW_SLOT)
        zt0 = jnp.swapaxes(wkv_prev.astype(F32), -1, -2)
        rwk, ztn = _rwkv(prw, shift_p, zt0, mu_p, vecs, w2_p, a2_p, g2_p, tt)
        x1, u2, route, cnt = _outproj(mla, rwk, x, mod_g, wo1, wo2, ln1, wr3, rb, gb, ts, u2_tail)
        shift_new = prw[:, s - 1:s, :RWKV_PROJ]
        return x1, u2, (route, cnt), kv_new, kr_new, shift_new, jnp.swapaxes(ztn, -1, -2)

    zero_shift = jnp.zeros((bp, 1, RWKV_PROJ), F32)
    zero_wkv = jnp.zeros((bp, RWKV_HEADS, RWKV_HEAD, RWKV_HEAD), F32)
    x1_s, u2_s, route_s, kv_s, kr_s, sh_s, wkv_s = mix_group(
        x_sample, mod_s, gb_s, ss, ss, ss, cache_kv_latent[0], jnp.swapaxes(cache_k_rope[0], 1, 2),
        state_shift[0], state_wkv[0])
    x1_p, u2_all, route_p, kv_p, kr_p, sh_p, wkv_p = mix_group(
        x_prompt, mod_p, 1, ROW_TILE, ATT_TQ, RWKV_TT, None, None, zero_shift, zero_wkv, u2_s)

    n_p_tok = bp * sp
    n_tok = n_p_tok + bs * ss
    code, n_valid, block_expert, n_used = _route_tables(jnp.concatenate([route_p[0], route_s[0]], axis=0),
                                                        jnp.concatenate([route_p[1], route_s[1]], axis=0))
    y = _moe(u2_all, code, n_valid, block_expert, n_used,
             expert_w_gate[0], expert_w_up[0], expert_w_down[0])

    out_p = _final(x1_p, y, route_p[0], mod_p, ln2, bp, sp, 1, ROW_TILE, 0, n_tok)
    out_s = _final(x1_s, y, route_s[0], mod_s, ln2, bs, ss, gb_s, ss, n_p_tok // ROW_TILE, n_tok)
    return (out_p, out_s, kv_p[None], kr_p[None], sh_p[None], wkv_p[None],
            kv_s[None], kr_s[None], sh_s[None], wkv_s[None])
```

```python
import functools

import jax
import jax.numpy as jnp
from jax import lax
from jax.experimental import pallas as pl
from jax.experimental.pallas import tpu as pltpu

F32 = jnp.float32
BF16 = jnp.bfloat16

LANES = 128
SUBLANES = 8

D_MODEL = 2048
CHUNK = 64
MLA_HEADS = 8
MLA_NOPE = 128
MLA_ROPE = 64
MLA_VDIM = 128
MLA_QK = MLA_NOPE + MLA_ROPE
MLA_WIDTH = MLA_HEADS * MLA_VDIM
Q_LORA = 512
KV_LORA = 512
RWKV_HEADS = 16
RWKV_HEAD = 64
RWKV_WIDTH = RWKV_HEADS * RWKV_HEAD
DECAY_LORA = 64
AAA_LORA = 64
GATE_LORA = 160
RWKV_PROJ = 3 * RWKV_WIDTH + DECAY_LORA + AAA_LORA + GATE_LORA
N_GROUPS = 8
EXPERTS_PER_GROUP = 8
N_EXPERTS = N_GROUPS * EXPERTS_PER_GROUP
TOP_K = 2
D_EXPERT = 512
MOE_BLOCK = 128
ROPE_THETA = 10000.0
LN_EPS = 1e-5
RMS_EPS = 1e-6
GN_EPS = 64e-5
DECAY_SCALE = 0.606531
SOFTMAX_SCALE = MLA_QK ** -0.5
DEEPNORM_ALPHA = 2.0 ** 0.25

MLA_SLOT = Q_LORA + KV_LORA + LANES
RW_LORA_SLOT = DECAY_LORA + AAA_LORA
RW_GATE_SLOT = 2 * LANES
RW_SLOT = 3 * RWKV_WIDTH + RW_LORA_SLOT + RW_GATE_SLOT
ROW_TILE = 512
ATT_TQ = 256
ATT_TK_PAST = 512
RWKV_TT = 128
PAIR = 2 * RWKV_HEAD
RWKV_HPG = 2
GRP = RWKV_HPG * RWKV_HEAD
VMEM_LIMIT = 56 * 1024 * 1024


def _dot(a, b):
    return jnp.dot(a.astype(BF16), b.astype(BF16), preferred_element_type=F32)


def _dot_nt(a, b):
    return lax.dot_general(a.astype(BF16), b.astype(BF16), (((1,), (1,)), ((), ())),
                           preferred_element_type=F32)


def _dot_tn(a, b):
    return lax.dot_general(a.astype(BF16), b.astype(BF16), (((0,), (0,)), ((), ())),
                           preferred_element_type=F32)


def _split3(x):
    h = x.astype(BF16)
    r1 = x - h.astype(F32)
    m = r1.astype(BF16)
    l = (r1 - m.astype(F32)).astype(BF16)
    return h, m, l


def _iota(shape, dim):
    return lax.broadcasted_iota(jnp.int32, shape, dim)


def _params(sem):
    return pltpu.CompilerParams(dimension_semantics=sem, vmem_limit_bytes=VMEM_LIMIT)


def _ada_kernel(c_ref, w_ref, b_ref, o_ref):
    c = c_ref[...]
    s = c * jax.nn.sigmoid(c)
    o_ref[...] = _dot(s, w_ref[...]) + b_ref[...]


def _ada(c_all, w_ada, b_ada):
    nb, d = c_all.shape
    n = w_ada.shape[1]
    tn = 1536
    return pl.pallas_call(
        _ada_kernel,
        grid=(n // tn,),
        in_specs=[pl.BlockSpec((nb, d), lambda j: (0, 0)),
                  pl.BlockSpec((d, tn), lambda j: (0, j)),
                  pl.BlockSpec((1, tn), lambda j: (0, j))],
        out_specs=pl.BlockSpec((nb, tn), lambda j: (0, j)),
        out_shape=jax.ShapeDtypeStruct((nb, n), F32),
        compiler_params=_params(("arbitrary",)),
    )(c_all, w_ada, b_ada.reshape(1, n))


def _inproj_kernel(x_ref, mod_ref, w_ref, om_ref, or_ref, u_ref):
    n = pl.program_id(2)
    gb, ts, d = x_ref.shape

    @pl.when(n == 0)
    def _():
        u = x_ref[...] * (1.0 + mod_ref[:, 1:2, :]) + mod_ref[:, 0:1, :]
        u_ref[...] = u.reshape(gb * ts, d).astype(BF16)

    res = jnp.dot(u_ref[...], w_ref[...], preferred_element_type=F32)

    @pl.when(n == 0)
    def _():
        om_ref[...] = res.reshape(om_ref.shape)

    @pl.when(n > 0)
    def _():
        or_ref[...] = res.reshape(or_ref.shape)


def _inproj(x, mod, w_packed, gb, ts):
    b, s, d = x.shape
    tn = MLA_SLOT
    nt = w_packed.shape[1] // tn
    return pl.pallas_call(
        _inproj_kernel,
        grid=(b // gb, s // ts, nt),
        in_specs=[pl.BlockSpec((gb, ts, d), lambda i, j, n: (i, j, 0)),
                  pl.BlockSpec((gb, 6, d), lambda i, j, n: (i, 0, 0)),
                  pl.BlockSpec((d, tn), lambda i, j, n: (0, n))],
        out_specs=[pl.BlockSpec((gb, ts, tn), lambda i, j, n: (i, j, 0)),
                   pl.BlockSpec((gb, ts, tn), lambda i, j, n: (i, j, jnp.maximum(n - 1, 0)))],
        out_shape=[jax.ShapeDtypeStruct((b, s, MLA_SLOT), F32),
                   jax.ShapeDtypeStruct((b, s, RW_SLOT), F32)],
        scratch_shapes=[pltpu.VMEM((gb * ts, d), BF16)],
        compiler_params=_params(("arbitrary", "arbitrary", "arbitrary")),
    )(x, mod, w_packed)


def _rms(x, g):
    return x * lax.rsqrt(jnp.mean(jnp.square(x), axis=-1, keepdims=True) + RMS_EPS) * g


def _rope_slot(slot, cos4, sin4):
    return slot * cos4 + pltpu.roll(slot, MLA_ROPE // 2, axis=1) * sin4


def _attn_kernel(*refs, tq, n_past, tk_past):
    if n_past:
        (pm_ref, cos_ref, sin_ref, gq_ref, gkv_ref, wuq_ref, wuk_ref, wuv_ref, ckv_ref, ckr_ref,
         mla_ref, kv_ref, kr_ref, qlat_ref, qrope_ref, m_ref, l_ref, acc_ref) = refs
    else:
        (pm_ref, cos_ref, sin_ref, gq_ref, gkv_ref, wuq_ref, wuk_ref, wuv_ref,
         mla_ref, kv_ref, kr_ref, qlat_ref, qrope_ref, m_ref, l_ref, acc_ref, kvs_ref, krs_ref) = refs
    j = pl.program_id(1)
    rows = MLA_HEADS * tq
    p = pm_ref[0]
    cos4 = cos_ref[...]
    sin4 = sin_ref[...]

    kv = _rms(p[:, Q_LORA:Q_LORA + KV_LORA], gkv_ref[...])
    kr = _rope_slot(p[:, Q_LORA + KV_LORA:MLA_SLOT], cos4, sin4)[:, :MLA_ROPE]
    kv_ref[0] = kv
    kr_ref[0] = kr
    kv_b = kv.astype(BF16)
    kr_b = kr.astype(BF16)

    q = _dot(_rms(p[:, :Q_LORA], gq_ref[...]), wuq_ref[...])
    for h in range(MLA_HEADS):
        qlat = _dot(q[:, h * MLA_NOPE:(h + 1) * MLA_NOPE], wuk_ref[h])
        rot = _rope_slot(q[:, MLA_WIDTH + h * LANES:MLA_WIDTH + (h + 1) * LANES], cos4, sin4)
        qlat_ref[h * tq:(h + 1) * tq, :] = (qlat * SOFTMAX_SCALE).astype(BF16)
        qrope_ref[h * tq:(h + 1) * tq, :] = (rot[:, :MLA_ROPE] * SOFTMAX_SCALE).astype(BF16)

    m_ref[...] = jnp.full(m_ref.shape, -jnp.inf, F32)
    l_ref[...] = jnp.zeros(l_ref.shape, F32)
    acc_ref[...] = jnp.zeros(acc_ref.shape, F32)

    grp_rows = max(tq, ATT_TQ)
    n_grp = rows // grp_rows

    def flash_step(kvb, krb, mask, kr_transposed=False):
        def scores(h):
            rs = slice(h * grp_rows, (h + 1) * grp_rows)
            rope = _dot(qrope_ref[rs, :], krb) if kr_transposed else _dot_nt(qrope_ref[rs, :], krb)
            return _dot_nt(qlat_ref[rs, :], kvb) + rope

        pending = None
        s_next = scores(0)
        for h in range(n_grp):
            rs = slice(h * grp_rows, (h + 1) * grp_rows)
            s = s_next
            if h + 1 < n_grp:
                s_next = scores(h + 1)
            if mask is not None:
                s = jnp.where(mask, s, -jnp.inf)
            m_prev = m_ref[rs, :]
            m_new = jnp.maximum(m_prev, jnp.max(s, axis=-1, keepdims=True))
            alpha = jnp.exp(m_prev - m_new)
            pr = jnp.exp(s - m_new)
            l_ref[rs, :] = alpha * l_ref[rs, :] + jnp.sum(pr, axis=-1, keepdims=True)
            m_ref[rs, :] = m_new
            pv = _dot(pr, kvb)
            if pending is not None:
                prs, palpha, ppv = pending
                acc_ref[prs, :] = palpha * acc_ref[prs, :] + ppv
            pending = (rs, alpha, pv)
        prs, palpha, ppv = pending
        acc_ref[prs, :] = palpha * acc_ref[prs, :] + ppv

    if n_past:
        def past_body(kb, carry):
            off = pl.multiple_of(kb * tk_past, tk_past)
            flash_step(ckv_ref[0, pl.ds(off, tk_past), :].astype(BF16),
                       ckr_ref[0, :, pl.ds(off, tk_past)].astype(BF16), None, kr_transposed=True)
            return carry
        lax.fori_loop(0, n_past // tk_past, past_body, 0)
        flash_step(kv_b, kr_b, None)
    else:
        off_j = pl.multiple_of(j * tq, tq)
        kvs_ref[pl.ds(off_j, tq), :] = kv_b
        krs_ref[pl.ds(off_j, tq), :] = kr_b

        def prev_body(kb, carry):
            off = pl.multiple_of(kb * 2 * tq, 2 * tq)
            flash_step(kvs_ref[pl.ds(off, 2 * tq), :], krs_ref[pl.ds(off, 2 * tq), :], None)
            return carry
        lax.fori_loop(0, j // 2, prev_body, 0)

        @pl.when(j % 2 == 0)
        def _():
            q_chunk = (_iota((grp_rows, tq), 0) & (tq - 1)) >> 6
            k_chunk = _iota((grp_rows, tq), 1) >> 6
            flash_step(kv_b, kr_b, k_chunk <= q_chunk)

        @pl.when(j % 2 == 1)
        def _():
            off = pl.multiple_of((j - 1) * tq, tq)
            q_chunk = ((_iota((grp_rows, 2 * tq), 0) & (tq - 1)) >> 6) + tq // CHUNK
            k_chunk = _iota((grp_rows, 2 * tq), 1) >> 6
            flash_step(kvs_ref[pl.ds(off, 2 * tq), :], krs_ref[pl.ds(off, 2 * tq), :], k_chunk <= q_chunk)

    o = acc_ref[...] / l_ref[...]
    for h in range(MLA_HEADS):
        mla_ref[0, :, h * MLA_VDIM:(h + 1) * MLA_VDIM] = _dot(o[h * tq:(h + 1) * tq, :], wuv_ref[h]).astype(BF16)


def _attn(pm, cos4, sin4, gq, gkv, wuq_p, wuk_t, wuv_t, cache_kv, cache_kr, tq):
    b, s, _ = pm.shape
    n_past = 0 if cache_kv is None else cache_kv.shape[1]
    rows = MLA_HEADS * tq
    const2 = lambda i, j: (0, 0)
    const3 = lambda i, j: (0, 0, 0)
    in_specs = [pl.BlockSpec((1, tq, MLA_SLOT), lambda i, j: (i, j, 0)),
                pl.BlockSpec((tq, LANES), lambda i, j: (j, 0)),
                pl.BlockSpec((tq, LANES), lambda i, j: (j, 0)),
                pl.BlockSpec((1, Q_LORA), const2),
                pl.BlockSpec((1, KV_LORA), const2),
                pl.BlockSpec(wuq_p.shape, const2),
                pl.BlockSpec(wuk_t.shape, const3),
                pl.BlockSpec(wuv_t.shape, const3)]
    args = [pm, cos4, sin4, gq, gkv, wuq_p, wuk_t, wuv_t]
    scratch = [pltpu.VMEM((rows, KV_LORA), BF16), pltpu.VMEM((rows, MLA_ROPE), BF16),
               pltpu.VMEM((rows, 1), F32), pltpu.VMEM((rows, 1), F32), pltpu.VMEM((rows, KV_LORA), F32)]
    if n_past:
        in_specs += [pl.BlockSpec((1, n_past, KV_LORA), lambda i, j: (i, 0, 0)),
                     pl.BlockSpec((1, MLA_ROPE, n_past), lambda i, j: (i, 0, 0))]
        args += [cache_kv, cache_kr]
    else:
        scratch += [pltpu.VMEM((s, KV_LORA), BF16), pltpu.VMEM((s, MLA_ROPE), BF16)]
    return pl.pallas_call(
        functools.partial(_attn_kernel, tq=tq, n_past=n_past, tk_past=min(ATT_TK_PAST, max(n_past, 1))),
        grid=(b, s // tq),
        in_specs=in_specs,
        out_specs=[pl.BlockSpec((1, tq, MLA_WIDTH), lambda i, j: (i, j, 0)),
                   pl.BlockSpec((1, tq, KV_LORA), lambda i, j: (i, j, 0)),
                   pl.BlockSpec((1, tq, MLA_ROPE), lambda i, j: (i, j, 0))],
        out_shape=[jax.ShapeDtypeStruct((b, s, MLA_WIDTH), BF16),
                   jax.ShapeDtypeStruct((b, s, KV_LORA), F32),
                   jax.ShapeDtypeStruct((b, s, MLA_ROPE), F32)],
        scratch_shapes=scratch,
        compiler_params=_params(("arbitrary", "arbitrary")),
    )(*args)


def _rwkv_kernel(pr_ref, shift_ref, zt0_ref, mu_ref, vec_ref, w2_ref, a2_ref, g2_ref,
                 out_ref, ztn_ref,
                 z_ref, carry_ref, y_ref, kt_ref, rt_ref, kh_ref, bh_ref, kb_ref, bb_ref, v_ref, wc_ref, *, tt):
    j = pl.program_id(1)
    nj = pl.num_programs(1)
    c = RWKV_WIDTH
    n_pairs = RWKV_HEADS // RWKV_HPG
    n_chunks = tt // CHUNK

    bd_mask = (_iota((PAIR, PAIR), 0) >> 6) == (_iota((PAIR, PAIR), 1) >> 6)
    bd_grp = (_iota((GRP, GRP), 0) >> 6) == (_iota((GRP, GRP), 1) >> 6)
    hsl = lambda hl: slice(hl * RWKV_HEAD, (hl + 1) * RWKV_HEAD)

    @pl.when(j == 0)
    def _():
        carry_ref[...] = jnp.broadcast_to(shift_ref[0], carry_ref.shape)
        for pr in range(n_pairs):
            z_ref[pr] = jnp.zeros((GRP, GRP), F32)
            for hl in range(RWKV_HPG):
                z_ref[pr, hsl(hl), hsl(hl)] = zt0_ref[0, RWKV_HPG * pr + hl]

    rw = pr_ref[0]
    row = _iota((tt, 1), 0)
    prev = jnp.where(row == 0, carry_ref[0:1, :], pltpu.roll(rw, 1, axis=0))
    carry_ref[...] = jnp.broadcast_to(rw[tt - 1:tt, :], carry_ref.shape)
    xm = rw + (prev - rw) * mu_ref[...]

    w0 = vec_ref[0:1, :]
    a0 = vec_ref[1:2, :]
    k_k = vec_ref[2:3, :]
    k_a = vec_ref[3:4, :]
    r_k = vec_ref[4:5, :]
    lnx_g = vec_ref[5:6, :]
    lnx_b = vec_ref[6:7, :]

    r = xm[:, 0:c]
    k = xm[:, c:2 * c]
    v = xm[:, 2 * c:3 * c]
    lora = xm[:, 3 * c:3 * c + RW_LORA_SLOT]
    gd = xm[:, 3 * c + RW_LORA_SLOT:RW_SLOT]
    lw = -DECAY_SCALE * jax.nn.sigmoid(w0 + _dot(jnp.tanh(lora), w2_ref[...]))
    a = jax.nn.sigmoid(a0 + _dot(lora, a2_ref[...]))
    g = _dot(jax.nn.sigmoid(gd), g2_ref[...])

    ones_bd = jnp.where(bd_mask, 1.0, 0.0).astype(BF16)

    def head_sum(x):
        h, m, _ = _split3(x)
        cols = []
        for blk in range(c // LANES):
            sl = slice(blk * LANES, (blk + 1) * LANES)
            cols.append(jnp.dot(h[:, sl], ones_bd, preferred_element_type=F32)
                        + jnp.dot(m[:, sl], ones_bd, preferred_element_type=F32))
        return jnp.concatenate(cols, axis=1)

    kk = k * k_k
    kk = kk / jnp.maximum(jnp.sqrt(head_sum(kk * kk)), 1e-12)
    k_mod = k * (1.0 + (a - 1.0) * k_a)
    bonus = head_sum(r * k_mod * r_k) * v
    b_vec = kk * a

    tri = ((_iota((tt, tt), 0) >> 6) == (_iota((tt, tt), 1) >> 6)) & (_iota((tt, tt), 1) <= _iota((tt, tt), 0))
    tri_b = jnp.where(tri, 1.0, 0.0).astype(BF16)
    lw_h, lw_m, lw_l = _split3(lw)
    cum = (jnp.dot(tri_b, lw_h, preferred_element_type=F32) + jnp.dot(tri_b, lw_m, preferred_element_type=F32)
           + jnp.dot(tri_b, lw_l, preferred_element_type=F32))
    cum_end = jnp.concatenate(
        [jnp.broadcast_to(cum[(ci + 1) * CHUNK - 1:(ci + 1) * CHUNK, :], (CHUNK, c)) for ci in range(n_chunks)],
        axis=0)
    for ci in range(n_chunks):
        wc_ref[ci * SUBLANES:(ci + 1) * SUBLANES, :] = jnp.broadcast_to(
            jnp.exp(cum[(ci + 1) * CHUNK - 1:(ci + 1) * CHUNK, :]), (SUBLANES, c))
    e_inv = jnp.exp(-cum)
    e_end = jnp.exp(cum_end - cum)
    kt_ref[...] = (kk * jnp.exp(cum - lw)).astype(BF16)
    rt_ref[...] = (r * jnp.exp(cum)).astype(BF16)
    kh_ref[...] = (k_mod * e_inv).astype(BF16)
    bh_ref[...] = (b_vec * e_inv).astype(BF16)
    kb_ref[...] = (k_mod * e_end).astype(BF16)
    bb_ref[...] = (b_vec * e_end).astype(BF16)
    v_ref[...] = v.astype(BF16)

    t_i = _iota((CHUNK, GRP), 0)
    j_i = _iota((CHUNK, GRP), 1) & (RWKV_HEAD - 1)
    strict2 = j_i < t_i
    incl2 = j_i <= t_i
    blk16 = (j_i >> 4) == (t_i >> 4)
    eye2 = jnp.where(j_i == t_i, 1.0, 0.0)
    zero_b = jnp.zeros((GRP, GRP), BF16)

    def stack2(x):
        xb = x.astype(BF16)
        return jnp.where(bd_grp, jnp.concatenate([xb] * RWKV_HPG, axis=0), zero_b)

    def pmm(x, y):
        return jnp.dot(x.astype(BF16), stack2(y), preferred_element_type=F32)

    items = [(ci, pr) for ci in range(n_chunks) for pr in range(n_pairs)]
    n_it = len(items)
    rsl = lambda ci: slice(ci * CHUNK, (ci + 1) * CHUNK)
    csl = lambda pr: slice(pr * GRP, (pr + 1) * GRP)
    each = lambda fn: [fn(i) for i in range(n_it)]

    kt = [kt_ref[rsl(ci), csl(pr)] for ci, pr in items]
    rt = [rt_ref[rsl(ci), csl(pr)] for ci, pr in items]
    vv = [v_ref[rsl(ci), csl(pr)] for ci, pr in items]
    khbh = [jnp.concatenate([stack2(kh_ref[rsl(ci), csl(pr)]), stack2(bh_ref[rsl(ci), csl(pr)])], axis=0)
            for ci, pr in items]
    a4 = each(lambda i: _dot_nt(jnp.concatenate([kt[i], rt[i]], axis=0), khbh[i]))
    m_mat = each(lambda i: jnp.where(strict2, a4[i][:CHUNK, :GRP], 0.0))
    n_mat = each(lambda i: jnp.where(strict2, a4[i][:CHUNK, GRP:], 0.0))
    p_mat = each(lambda i: jnp.where(incl2, a4[i][CHUNK:, :GRP], 0.0))
    q_mat = each(lambda i: jnp.where(incl2, a4[i][CHUNK:, GRP:], 0.0))
    n_d = each(lambda i: jnp.where(blk16, n_mat[i], 0.0))
    n_o = each(lambda i: n_mat[i] - n_d[i])
    n2 = each(lambda i: pmm(n_d[i], n_d[i]))
    t1 = each(lambda i: pmm(eye2 - n_d[i], eye2 + n2[i]))
    n4 = each(lambda i: pmm(n2[i], n2[i]))
    t2 = each(lambda i: pmm(t1[i], eye2 + n4[i]))
    n8 = each(lambda i: pmm(n4[i], n4[i]))
    t_d = each(lambda i: pmm(t2[i], eye2 + n8[i]))
    x1 = each(lambda i: pmm(t_d[i], n_o[i]))
    x2 = each(lambda i: pmm(x1[i], x1[i]))
    t3 = each(lambda i: pmm(eye2 - x1[i], eye2 + x2[i]))
    t_mat = each(lambda i: pmm(t3[i], t_d[i]))
    mvpv = each(lambda i: jnp.dot(jnp.concatenate([m_mat[i], p_mat[i]], axis=0).astype(BF16), stack2(vv[i]),
                                  preferred_element_type=F32))
    tkmv = each(lambda i: jnp.dot(t_mat[i].astype(BF16),
                                  jnp.concatenate([stack2(kt[i]), stack2(mvpv[i][:CHUNK])], axis=1),
                                  preferred_element_type=F32))
    z = [z_ref[pr] for pr in range(n_pairs)]
    for ci in range(n_chunks):
        ids = [ci * n_pairs + pr for pr in range(n_pairs)]
        zr = [jnp.dot(jnp.concatenate([tkmv[i][:, :GRP].astype(BF16), rt[i]], axis=0), z[pr].astype(BF16),
                      preferred_element_type=F32) for pr, i in enumerate(ids)]
        sk = [zr[pr][:CHUNK] + tkmv[i][:, GRP:] for pr, i in enumerate(ids)]
        qs = [pmm(q_mat[i], sk[pr]) for pr, i in enumerate(ids)]
        upd = [_dot_tn(jnp.concatenate([kb_ref[rsl(ci), csl(pr)], bb_ref[rsl(ci), csl(pr)]], axis=0),
                       jnp.concatenate([vv[i], (-sk[pr]).astype(BF16)], axis=0)) for pr, i in enumerate(ids)]
        for pr, i in enumerate(ids):
            y_ref[rsl(ci), csl(pr)] = zr[pr][CHUNK:] + mvpv[i][CHUNK:] - qs[pr]
            wc_col = jnp.transpose(wc_ref[ci * SUBLANES:(ci + 1) * SUBLANES, csl(pr)])[:, 0:1]
            z[pr] = z[pr] * wc_col + jnp.where(bd_grp, upd[pr], 0.0)
    for pr in range(n_pairs):
        z_ref[pr] = z[pr]

    y = y_ref[...]
    mean = head_sum(y) * (1.0 / RWKV_HEAD)
    dlt = y - mean
    var = head_sum(dlt * dlt) * (1.0 / RWKV_HEAD)
    yn = dlt * lax.rsqrt(var + GN_EPS) * lnx_g + lnx_b
    out_ref[0] = ((yn + bonus) * g).astype(BF16)

    @pl.when(j == nj - 1)
    def _():
        for pr in range(n_pairs):
            for hl in range(RWKV_HPG):
                ztn_ref[0, RWKV_HPG * pr + hl] = z_ref[pr, hsl(hl), hsl(hl)]


def _rwkv(pr, shift_p, zt0, mu_p, vecs, w2_p, a2_p, g2_p, tt):
    b, s, _ = pr.shape
    c = RWKV_WIDTH
    const2 = lambda i, j: (0, 0)
    full = lambda shape: pl.BlockSpec(shape, const2)
    return pl.pallas_call(
        functools.partial(_rwkv_kernel, tt=tt),
        grid=(b, s // tt),
        in_specs=[pl.BlockSpec((1, tt, RW_SLOT), lambda i, j: (i, j, 0)),
                  pl.BlockSpec((1, 1, RW_SLOT), lambda i, j: (i, 0, 0)),
                  pl.BlockSpec((1, RWKV_HEADS, RWKV_HEAD, RWKV_HEAD), lambda i, j: (i, 0, 0, 0)),
                  full((1, RW_SLOT)), full((SUBLANES, c)),
                  full(w2_p.shape), full(a2_p.shape), full(g2_p.shape)],
        out_specs=[pl.BlockSpec((1, tt, c), lambda i, j: (i, j, 0)),
                   pl.BlockSpec((1, RWKV_HEADS, RWKV_HEAD, RWKV_HEAD), lambda i, j: (i, 0, 0, 0))],
        out_shape=[jax.ShapeDtypeStruct((b, s, c), BF16),
                   jax.ShapeDtypeStruct((b, RWKV_HEADS, RWKV_HEAD, RWKV_HEAD), F32)],
        scratch_shapes=[pltpu.VMEM((RWKV_HEADS // RWKV_HPG, GRP, GRP), F32),
                        pltpu.VMEM((SUBLANES, RW_SLOT), F32),
                        pltpu.VMEM((tt, c), F32)]
                       + [pltpu.VMEM((tt, c), BF16) for _ in range(7)]
                       + [pltpu.VMEM((SUBLANES * (tt // CHUNK), c), F32)],
        compiler_params=_params(("arbitrary", "arbitrary")),
    )(pr, shift_p, zt0, mu_p, vecs, w2_p, a2_p, g2_p)


def _layer_norm(h, g, b):
    mu = jnp.mean(h, axis=-1, keepdims=True)
    d = h - mu
    var = jnp.mean(d * d, axis=-1, keepdims=True)
    return d * lax.rsqrt(var + LN_EPS) * g + b


def _store_token_major(ref, x):
    n = x.shape[0]
    for c in range(D_MODEL // LANES):
        ref[pl.ds(c, n, stride=D_MODEL // LANES), :] = x[:, c * LANES:(c + 1) * LANES]


def _load_token_major(ref, n, lead=()):
    cols = [ref[lead + (pl.ds(c, n, stride=D_MODEL // LANES), slice(None))] for c in range(D_MODEL // LANES)]
    return jnp.concatenate(cols, axis=1)


def _outproj_kernel(*refs, n_own):
    if n_own is None:
        _outproj_tile(*refs)
        return
    tail_ref = refs[9]
    t = pl.program_id(0)

    @pl.when(t < n_own)
    def _():
        _outproj_tile(*refs[:9], *refs[10:])

    @pl.when(t >= n_own)
    def _():
        refs[11][...] = tail_ref[...]


def _outproj_tile(mla_ref, rwk_ref, x_ref, mod_ref, wo1_ref, wo2_ref, ln_ref, wr_ref, rb_ref,
                  x1_ref, u2_ref, route_ref, cnt_ref):
    gb, ts, d = x_ref.shape
    rows = gb * ts
    mix = (jnp.dot(mla_ref[...].reshape(rows, MLA_WIDTH), wo1_ref[...], preferred_element_type=F32)
           + jnp.dot(rwk_ref[...].reshape(rows, RWKV_WIDTH), wo2_ref[...], preferred_element_type=F32))
    h = DEEPNORM_ALPHA * x_ref[...] + mod_ref[:, 2:3, :] * mix.reshape(gb, ts, d)
    x1 = _layer_norm(h, ln_ref[0:1, :], ln_ref[1:2, :])
    u2 = (x1 * (1.0 + mod_ref[:, 4:5, :]) + mod_ref[:, 3:4, :]).reshape(rows, d)
    x1_ref[...] = x1.reshape(rows, d)
    _store_token_major(u2_ref, u2)

    uh, um, _ = _split3(u2)
    lg = (jnp.dot(uh, wr_ref[0], preferred_element_type=F32) + jnp.dot(um, wr_ref[0], preferred_element_type=F32)
          + jnp.dot(uh, wr_ref[1], preferred_element_type=F32)) + rb_ref[...]
    lane = _iota((rows, LANES), 1).astype(F32)
    neg = -jnp.inf
    big = float(LANES)
    gl = jnp.where(lane < N_GROUPS, lg, neg)
    gmax = jnp.max(gl, axis=-1, keepdims=True)
    p_grp = 1.0 / jnp.sum(jnp.exp(gl - gmax), axis=-1, keepdims=True)
    grp = jnp.min(jnp.where(gl == gmax, lane, big), axis=-1, keepdims=True)
    lo = N_GROUPS + grp * EXPERTS_PER_GROUP
    el = jnp.where((lane >= lo) & (lane < lo + EXPERTS_PER_GROUP), lg, neg)
    v1 = jnp.max(el, axis=-1, keepdims=True)
    i1 = jnp.min(jnp.where(el == v1, lane, big), axis=-1, keepdims=True)
    el2 = jnp.where(lane == i1, neg, el)
    v2 = jnp.max(el2, axis=-1, keepdims=True)
    i2 = jnp.min(jnp.where(el2 == v2, lane, big), axis=-1, keepdims=True)
    e2 = jnp.exp(v2 - v1)
    den = 1.0 / (1.0 + e2)
    route_ref[...] = jnp.where(lane == 0, i1 - N_GROUPS,
                               jnp.where(lane == 1, i2 - N_GROUPS,
                                         jnp.where(lane == 2, p_grp * den,
                                                   jnp.where(lane == 3, p_grp * e2 * den, 0.0))))
    hits = jnp.where((lane == i1 - N_GROUPS) | (lane == i2 - N_GROUPS), 1.0, 0.0)
    cnt_ref[0] = jnp.broadcast_to(jnp.sum(hits, axis=0, keepdims=True), (SUBLANES, LANES))


def _outproj(mla, rwk, x, mod, wo1, wo2, ln, wr3, rb, gb, ts, u2_tail=None):
    b, s, d = x.shape
    rows = gb * ts
    nj = s // ts
    n_own = (b // gb) * nj
    lg = d // LANES
    n_tail = 0 if u2_tail is None else u2_tail.shape[0] // (rows * lg)
    own = lambda t: jnp.minimum(t, n_own - 1)
    seq3 = lambda t: (own(t) // nj, own(t) % nj, 0)
    const2 = lambda t: (0, 0)
    once = pl.Buffered(1)
    in_specs = [pl.BlockSpec((gb, ts, MLA_WIDTH), seq3),
                pl.BlockSpec((gb, ts, RWKV_WIDTH), seq3),
                pl.BlockSpec((gb, ts, d), seq3),
                pl.BlockSpec((gb, 6, d), lambda t: (own(t) // nj, 0, 0)),
                pl.BlockSpec(wo1.shape, const2, pipeline_mode=once),
                pl.BlockSpec(wo2.shape, const2, pipeline_mode=once),
                pl.BlockSpec(ln.shape, const2),
                pl.BlockSpec(wr3.shape, lambda t: (0, 0, 0), pipeline_mode=once),
                pl.BlockSpec(rb.shape, const2)]
    args = [mla, rwk, x, mod, wo1, wo2, ln, wr3, rb]
    if n_tail:
        in_specs.append(pl.BlockSpec((rows * lg, LANES), lambda t: (jnp.maximum(t - n_own, 0), 0)))
        args.append(u2_tail)
    return pl.pallas_call(
        functools.partial(_outproj_kernel, n_own=n_own if n_tail else None),
        grid=(n_own + n_tail,),
        in_specs=in_specs,
        out_specs=[pl.BlockSpec((rows, d), lambda t: (own(t), 0)),
                   pl.BlockSpec((rows * lg, LANES), lambda t: (t, 0)),
                   pl.BlockSpec((rows, LANES), lambda t: (own(t), 0)),
                   pl.BlockSpec((1, SUBLANES, LANES), lambda t: (own(t), 0, 0))],
        out_shape=[jax.ShapeDtypeStruct((b * s, d), F32),
                   jax.ShapeDtypeStruct(((n_own + n_tail) * rows * lg, LANES), F32),
                   jax.ShapeDtypeStruct((b * s, LANES), F32),
                   jax.ShapeDtypeStruct((n_own, SUBLANES, LANES), F32)],
        compiler_params=_params(("arbitrary",)),
    )(*args)


def _moe_kernel(bexp_ref, code_ref, nval_ref, nused_ref,
                u2_hbm, wg_ref, wu_ref, wd_ref,
                y_hbm,
                xbuf, ybuf, wgb, wub, wdb, gsem, ssem, *, n_tok):
    i = pl.program_id(0)
    nused = nused_ref[0]
    slot = i % 3

    lg = D_MODEL // LANES

    def gather_copy(src_tok, xs, r):
        return pltpu.make_async_copy(u2_hbm.at[pl.ds(pl.multiple_of(src_tok * lg, lg), lg)],
                                     xbuf.at[xs, pl.ds(r * lg, lg)], gsem.at[xs])

    def scatter_copy(blk, r, for_wait=False):
        if for_wait:
            dst = 0
        else:
            code = code_ref[blk * MOE_BLOCK + r]
            dst = ((code >> 1) & 1) * n_tok + (code >> 2)
        return pltpu.make_async_copy(ybuf.at[blk % 3, pl.ds(r * lg, lg)],
                                     y_hbm.at[pl.ds(pl.multiple_of(dst * lg, lg), lg)], ssem.at[blk % 3])

    def issue_gather(blk, xs):
        for r in range(MOE_BLOCK):
            gather_copy(code_ref[blk * MOE_BLOCK + r] >> 2, xs, r).start()

    def wait_gather(xs):
        def body(r, carry):
            gather_copy(0, xs, r).wait()
            return carry
        lax.fori_loop(0, MOE_BLOCK, body, 0, unroll=8)

    def scatter_partial(blk, start):
        def body(r, carry):
            if start:
                scatter_copy(blk, r).start()
            else:
                scatter_copy(blk, r, for_wait=True).wait()
            return carry
        lax.fori_loop(0, nval_ref[blk], body, 0)

    def wait_scatter(blk):
        @pl.when(nval_ref[blk] == MOE_BLOCK)
        def _():
            def body(r, carry):
                scatter_copy(blk, r, for_wait=True).wait()
                return carry
            lax.fori_loop(0, MOE_BLOCK, body, 0, unroll=8)

        @pl.when(nval_ref[blk] < MOE_BLOCK)
        def _():
            scatter_partial(blk, False)

    def compute(with_prev_scatter):
        issue_gather(jnp.minimum(i + 2, nused - 1), (i + 2) % 3)
        if with_prev_scatter:
            for r in range(MOE_BLOCK):
                scatter_copy(i - 1, r).start()
        xb = _load_token_major(xbuf, MOE_BLOCK, (slot,)).astype(BF16)
        hg = jnp.dot(xb, wgb[...], preferred_element_type=F32)
        hu = jnp.dot(xb, wub[...], preferred_element_type=F32)
        hh = (hg * jax.nn.sigmoid(hg) * hu).astype(BF16)
        _store_token_major(ybuf.at[i % 3], jnp.dot(hh, wdb[...], preferred_element_type=F32))

    @pl.when(i == 0)
    def _():
        issue_gather(0, 0)
        issue_gather(jnp.minimum(1, nused - 1), 1)

    @pl.when(i <= nused)
    def _():
        wait_gather(slot)

        @pl.when(i >= 3)
        def _():
            wait_scatter(i - 3)

    @pl.when(i < nused)
    def _():
        prev_e = bexp_ref[jnp.maximum(i - 1, 0)]

        @pl.when((i == 0) | (bexp_ref[i] != prev_e))
        def _():
            wgb[...] = wg_ref[0].astype(BF16)
            wub[...] = wu_ref[0].astype(BF16)
            wdb[...] = wd_ref[0].astype(BF16)

        prev_full = (i >= 1) & (nval_ref[jnp.maximum(i - 1, 0)] == MOE_BLOCK)

        @pl.when(prev_full)
        def _():
            compute(True)

        @pl.when(jnp.logical_not(prev_full))
        def _():
            compute(False)

            @pl.when(i >= 1)
            def _():
                scatter_partial(i - 1, True)

    @pl.when(i == nused)
    def _():
        wait_gather((i + 1) % 3)
        scatter_partial(i - 1, True)

        @pl.when(i >= 2)
        def _():
            wait_scatter(i - 2)
        wait_scatter(i - 1)


def _moe(u2_all, code, n_valid, block_expert, n_used, wg, wu, wd):
    d = wg.shape[1]
    lg = d // LANES
    n_tok = u2_all.shape[0] // lg
    n_blocks = block_expert.shape[0]
    de = wg.shape[2]

    def wmap(i, bexp, code_r, nval, nused):
        return (bexp[jnp.minimum(i, nused[0] - 1)], 0, 0)

    grid_spec = pltpu.PrefetchScalarGridSpec(
        num_scalar_prefetch=4,
        grid=(n_blocks + 1,),
        in_specs=[pl.BlockSpec(memory_space=pl.ANY),
                  pl.BlockSpec((1, d, de), wmap),
                  pl.BlockSpec((1, d, de), wmap),
                  pl.BlockSpec((1, de, d), wmap)],
        out_specs=pl.BlockSpec(memory_space=pl.ANY),
        scratch_shapes=[pltpu.VMEM((3, MOE_BLOCK * lg, LANES), F32), pltpu.VMEM((3, MOE_BLOCK * lg, LANES), F32),
                        pltpu.VMEM((d, de), BF16), pltpu.VMEM((d, de), BF16), pltpu.VMEM((de, d), BF16),
                        pltpu.SemaphoreType.DMA((3,)), pltpu.SemaphoreType.DMA((3,))])
    return pl.pallas_call(
        functools.partial(_moe_kernel, n_tok=n_tok),
        grid_spec=grid_spec,
        out_shape=jax.ShapeDtypeStruct((TOP_K * n_tok * lg, LANES), F32),
        compiler_params=_params(("arbitrary",)),
    )(block_expert, code, n_valid, n_used, u2_all, wg, wu, wd)


def _rank_kernel(route_ref, base_ref, o_ref):
    rows = route_ref.shape[0]
    lane = _iota((rows, LANES), 1).astype(F32)
    hit1 = lane == route_ref[:, 0:1]
    hit2 = lane == route_ref[:, 1:2]
    hits = jnp.where(hit1 | hit2, 1.0, 0.0).astype(BF16)
    earlier = jnp.where(_iota((rows, rows), 1) < _iota((rows, rows), 0), 1.0, 0.0).astype(BF16)
    pos = jnp.dot(earlier, hits, preferred_element_type=F32) + base_ref[0, 0:1, :]
    d1 = jnp.sum(jnp.where(hit1, pos, 0.0), axis=-1, keepdims=True)
    d2 = jnp.sum(jnp.where(hit2, pos, 0.0), axis=-1, keepdims=True)
    o_ref[...] = jnp.where(lane == 0, d1, jnp.where(lane == 1, d2, 0.0))


def _route_tables(route, tile_counts):
    n_tok = route.shape[0]
    n_asg = n_tok * TOP_K
    n_tiles = tile_counts.shape[0]
    rows = n_tok // n_tiles
    tc = tile_counts[:, 0, :].astype(jnp.int32)
    counts = jnp.sum(tc, axis=0)
    padded = (counts + MOE_BLOCK - 1) // MOE_BLOCK * MOE_BLOCK
    pad_ends = jnp.cumsum(padded)
    pad_starts = pad_ends - padded
    base = (pad_starts[None, :] + jnp.cumsum(tc, axis=0) - tc).astype(F32)
    base = jnp.broadcast_to(base[:, None, :], (n_tiles, SUBLANES, LANES))
    dest = pl.pallas_call(
        _rank_kernel,
        grid=(n_tiles,),
        in_specs=[pl.BlockSpec((rows, LANES), lambda i: (i, 0)),
                  pl.BlockSpec((1, SUBLANES, LANES), lambda i: (i, 0, 0))],
        out_specs=pl.BlockSpec((rows, LANES), lambda i: (i, 0)),
        out_shape=jax.ShapeDtypeStruct((n_tok, LANES), F32),
        compiler_params=_params(("arbitrary",)),
    )(route, base)
    dest = dest[:, 0:TOP_K].astype(jnp.int32).reshape(n_asg)
    n_blocks = -(-(n_asg + N_EXPERTS * (MOE_BLOCK - 1)) // MOE_BLOCK)
    n_rows = n_blocks * MOE_BLOCK
    code = jnp.zeros((n_rows,), jnp.int32).at[dest].set(
        jnp.arange(n_asg, dtype=jnp.int32) * 2 + 1, unique_indices=True)
    blk_start = jnp.arange(n_blocks, dtype=jnp.int32) * MOE_BLOCK
    block_expert = jnp.minimum(
        jnp.sum((pad_ends[None, :N_EXPERTS] <= blk_start[:, None]).astype(jnp.int32), axis=1),
        N_EXPERTS - 1).astype(jnp.int32)
    n_used = (pad_ends[N_EXPERTS - 1:N_EXPERTS] // MOE_BLOCK).astype(jnp.int32)
    filled = (pad_starts + counts)[:N_EXPERTS]
    n_valid = jnp.where(blk_start < pad_ends[N_EXPERTS - 1],
                        jnp.clip(filled[block_expert] - blk_start, 0, MOE_BLOCK), 0).astype(jnp.int32)
    return code, n_valid, block_expert, n_used


def _final_kernel(x1_ref, y0_ref, y1_ref, route_ref, mod_ref, ln_ref, o_ref):
    gb, ts, d = o_ref.shape
    y0 = _load_token_major(y0_ref, gb * ts)
    y1 = _load_token_major(y1_ref, gb * ts)
    moe = (route_ref[:, 2:3] * y0 + route_ref[:, 3:4] * y1).reshape(gb, ts, d)
    h = DEEPNORM_ALPHA * x1_ref[...].reshape(gb, ts, d) + mod_ref[:, 5:6, :] * moe
    o_ref[...] = _layer_norm(h, ln_ref[0:1, :], ln_ref[1:2, :])


def _final(x1, y, route, mod, ln, b, s, gb, ts, tile0, n_tok):
    d = x1.shape[1]
    rows = gb * ts
    nj = s // ts
    k_off = n_tok // rows
    return pl.pallas_call(
        _final_kernel,
        grid=(b // gb, nj),
        in_specs=[pl.BlockSpec((rows, d), lambda i, j: (i * nj + j, 0)),
                  pl.BlockSpec((rows * (d // LANES), LANES), lambda i, j: (tile0 + i * nj + j, 0)),
                  pl.BlockSpec((rows * (d // LANES), LANES), lambda i, j: (k_off + tile0 + i * nj + j, 0)),
                  pl.BlockSpec((rows, LANES), lambda i, j: (i * nj + j, 0)),
                  pl.BlockSpec((gb, 6, d), lambda i, j: (i, 0, 0)),
                  pl.BlockSpec(ln.shape, lambda i, j: (0, 0))],
        out_specs=pl.BlockSpec((gb, ts, d), lambda i, j: (i, j, 0)),
        out_shape=jax.ShapeDtypeStruct((b, s, d), F32),
        compiler_params=_params(("arbitrary", "arbitrary")),
    )(x1, y, y, route, mod, ln)


def _rope_tables(n_past, s):
    inv_freq = ROPE_THETA ** (-jnp.arange(0, MLA_ROPE, 2, dtype=F32) / MLA_ROPE)
    ang = jnp.arange(n_past, n_past + s).astype(F32)[:, None] * inv_freq[None, :]
    cos, sin = jnp.cos(ang), jnp.sin(ang)
    return jnp.concatenate([cos, cos, cos, cos], axis=1), jnp.concatenate([-sin, sin, -sin, sin], axis=1)


def _pad_cols(w, n):
    return jnp.pad(w, ((0, 0), (0, n - w.shape[1])))


def kernel(x_prompt, x_sample, c_prompt, c_sample, cache_kv_latent, cache_k_rope, state_shift, state_wkv, w_ada, b_ada, w_in, q_norm_g, w_uq, kv_norm_g, w_uk, w_uv, rwkv_mu, rwkv_w0, rwkv_w2, rwkv_a0, rwkv_a2, rwkv_g2, rwkv_k_k, rwkv_k_a, rwkv_r_k, rwkv_lnx_g, rwkv_lnx_b, w_out, ln1_g, ln1_b, router_group_w, router_group_b, router_expert_w, router_expert_b, expert_w_gate, expert_w_up, expert_w_down, ln2_g, ln2_b):
    depth = w_in.shape[0]
    assert depth == 1
    bp, sp, d = x_prompt.shape
    bs, ss, _ = x_sample.shape
    n_past = cache_kv_latent.shape[2]
    assert d == D_MODEL and sp % ROW_TILE == 0 and ss == CHUNK and bs % (ROW_TILE // CHUNK) == 0
    assert sp % ATT_TQ == 0 and n_past % ATT_TK_PAST == 0
    gb_s = ROW_TILE // ss
    mla_proj = Q_LORA + KV_LORA + MLA_ROPE

    wi = w_in[0]
    w_in_p = jnp.concatenate(
        [wi[:, :mla_proj], wi[:, mla_proj - MLA_ROPE:mla_proj], _pad_cols(wi[:, mla_proj:], RW_SLOT)],
        axis=1).astype(BF16)
    wq = w_uq[0].reshape(Q_LORA, MLA_HEADS, MLA_QK)
    wq_rope = wq[:, :, MLA_NOPE:]
    wuq_p = jnp.concatenate(
        [wq[:, :, :MLA_NOPE].reshape(Q_LORA, MLA_WIDTH),
         jnp.concatenate([wq_rope, wq_rope], axis=2).reshape(Q_LORA, MLA_HEADS * LANES)], axis=1).astype(BF16)
    wuk_t = jnp.transpose(w_uk[0], (1, 2, 0)).astype(BF16)
    wuv_t = jnp.transpose(w_uv[0], (1, 0, 2)).astype(BF16)
    gq = q_norm_g[0].reshape(1, Q_LORA)
    gkv = kv_norm_g[0].reshape(1, KV_LORA)
    mu_p = _pad_cols(rwkv_mu[0].reshape(1, RWKV_PROJ), RW_SLOT)
    zeros_c = jnp.zeros((RWKV_WIDTH,), F32)
    vecs = jnp.stack([rwkv_w0[0], rwkv_a0[0], rwkv_k_k[0], rwkv_k_a[0], rwkv_r_k[0].reshape(RWKV_WIDTH),
                      rwkv_lnx_g[0], rwkv_lnx_b[0], zeros_c])
    w2_p = jnp.concatenate([rwkv_w2[0], jnp.zeros((AAA_LORA, RWKV_WIDTH), F32)], axis=0).astype(BF16)
    a2_p = jnp.concatenate([jnp.zeros((DECAY_LORA, RWKV_WIDTH), F32), rwkv_a2[0]], axis=0).astype(BF16)
    g2_p = jnp.concatenate([rwkv_g2[0], jnp.zeros((RW_GATE_SLOT - GATE_LORA, RWKV_WIDTH), F32)],
                           axis=0).astype(BF16)
    wo1 = w_out[0][:MLA_WIDTH].astype(BF16)
    wo2 = w_out[0][MLA_WIDTH:].astype(BF16)
    ln1 = jnp.stack([ln1_g[0], ln1_b[0]])
    ln2 = jnp.stack([ln2_g[0], ln2_b[0]])
    wr = _pad_cols(jnp.concatenate([router_group_w[0], router_expert_w[0]], axis=1), LANES)
    wr3 = jnp.stack(_split3(wr))
    rb = _pad_cols(jnp.concatenate([router_group_b[0], router_expert_b[0]]).reshape(1, -1), LANES)

    mod = _ada(jnp.concatenate([c_prompt, c_sample], axis=0), w_ada[0], b_ada[0]).reshape(bp + bs, 6, d)
    mod_p, mod_s = mod[:bp], mod[bp:]

    def mix_group(x, mod_g, gb, ts, tq, tt, cache_kv, cache_kr, shift_prev, wkv_prev, u2_tail=None):
        b, s, _ = x.shape
        pm, prw = _inproj(x, mod_g, w_in_p, gb, ts)
        n_p = 0 if cache_kv is None else cache_kv.shape[1]
        cos4, sin4 = _rope_tables(n_p, s)
        mla, kv_new, kr_new = _attn(pm, cos4, sin4, gq, gkv, wuq_p, wuk_t, wuv_t, cache_kv, cache_kr, tq)
        shift_p = _pad_cols(shift_prev.reshape(b, RWKV_PROJ), RW_SLOT).reshape(b, 1, RW_SLOT)
        zt0 = jnp.swapaxes(wkv_prev.astype(F32), -1, -2)
        rwk, ztn = _rwkv(prw, shift_p, zt0, mu_p, vecs, w2_p, a2_p, g2_p, tt)
        x1, u2, route, cnt = _outproj(mla, rwk, x, mod_g, wo1, wo2, ln1, wr3, rb, gb, ts, u2_tail)
        shift_new = prw[:, s - 1:s, :RWKV_PROJ]
        return x1, u2, (route, cnt), kv_new, kr_new, shift_new, jnp.swapaxes(ztn, -1, -2)

    zero_shift = jnp.zeros((bp, 1, RWKV_PROJ), F32)
    zero_wkv = jnp.zeros((bp, RWKV_HEADS, RWKV_HEAD, RWKV_HEAD), F32)
    x1_s, u2_s, route_s, kv_s, kr_s, sh_s, wkv_s = mix_group(
        x_sample, mod_s, gb_s, ss, ss, ss, cache_kv_latent[0], jnp.swapaxes(cache_k_rope[0], 1, 2),
        state_shift[0], state_wkv[0])
    x1_p, u2_all, route_p, kv_p, kr_p, sh_p, wkv_p = mix_group(
        x_prompt, mod_p, 1, ROW_TILE, ATT_TQ, RWKV_TT, None, None, zero_shift, zero_wkv, u2_s)

    n_p_tok = bp * sp
    n_tok = n_p_tok + bs * ss
    code, n_valid, block_expert, n_used = _route_tables(jnp.concatenate([route_p[0], route_s[0]], axis=0),
                                                        jnp.concatenate([route_p[1], route_s[1]], axis=0))
    y = _moe(u2_all, code, n_valid, block_expert, n_used,
             expert_w_gate[0], expert_w_up[0], expert_w_down[0])

    out_p = _final(x1_p, y, route_p[0], mod_p, ln2, bp, sp, 1, ROW_TILE, 0, n_tok)
    out_s = _final(x1_s, y, route_s[0], mod_s, ln2, bs, ss, gb_s, ss, n_p_tok // ROW_TILE, n_tok)
    return (out_p, out_s, kv_p[None], kr_p[None], sh_p[None], wkv_p[None],
            kv_s[None], kr_s[None], sh_s[None], wkv_s[None])
```

```python
import functools

import jax
import jax.numpy as jnp
from jax import lax
from jax.experimental import pallas as pl
from jax.experimental.pallas import tpu as pltpu

F32 = jnp.float32
BF16 = jnp.bfloat16

LANES = 128
SUBLANES = 8

D_MODEL = 2048
CHUNK = 64
MLA_HEADS = 8
MLA_NOPE = 128
MLA_ROPE = 64
MLA_VDIM = 128
MLA_QK = MLA_NOPE + MLA_ROPE
MLA_WIDTH = MLA_HEADS * MLA_VDIM
Q_LORA = 512
KV_LORA = 512
RWKV_HEADS = 16
RWKV_HEAD = 64
RWKV_WIDTH = RWKV_HEADS * RWKV_HEAD
DECAY_LORA = 64
AAA_LORA = 64
GATE_LORA = 160
RWKV_PROJ = 3 * RWKV_WIDTH + DECAY_LORA + AAA_LORA + GATE_LORA
N_GROUPS = 8
EXPERTS_PER_GROUP = 8
N_EXPERTS = N_GROUPS * EXPERTS_PER_GROUP
TOP_K = 2
D_EXPERT = 512
MOE_BLOCK = 128
ROPE_THETA = 10000.0
LN_EPS = 1e-5
RMS_EPS = 1e-6
GN_EPS = 64e-5
DECAY_SCALE = 0.606531
SOFTMAX_SCALE = MLA_QK ** -0.5
DEEPNORM_ALPHA = 2.0 ** 0.25

MLA_SLOT = Q_LORA + KV_LORA + LANES
RW_LORA_SLOT = DECAY_LORA + AAA_LORA
RW_GATE_SLOT = 2 * LANES
RW_SLOT = 3 * RWKV_WIDTH + RW_LORA_SLOT + RW_GATE_SLOT
ROW_TILE = 512
ATT_TQ = 256
ATT_TK_PAST = 512
RWKV_TT = 128
PAIR = 2 * RWKV_HEAD
RWKV_HPG = 2
GRP = RWKV_HPG * RWKV_HEAD
VMEM_LIMIT = 56 * 1024 * 1024


def _dot(a, b):
    return jnp.dot(a.astype(BF16), b.astype(BF16), preferred_element_type=F32)


def _dot_nt(a, b):
    return lax.dot_general(a.astype(BF16), b.astype(BF16), (((1,), (1,)), ((), ())),
                           preferred_element_type=F32)


def _dot_tn(a, b):
    return lax.dot_general(a.astype(BF16), b.astype(BF16), (((0,), (0,)), ((), ())),
                           preferred_element_type=F32)


def _split3(x):
    h = x.astype(BF16)
    r1 = x - h.astype(F32)
    m = r1.astype(BF16)
    l = (r1 - m.astype(F32)).astype(BF16)
    return h, m, l


def _iota(shape, dim):
    return lax.broadcasted_iota(jnp.int32, shape, dim)


def _params(sem):
    return pltpu.CompilerParams(dimension_semantics=sem, vmem_limit_bytes=VMEM_LIMIT)


def _ada_kernel(c_ref, w_ref, b_ref, o_ref):
    c = c_ref[...]
    s = c * jax.nn.sigmoid(c)
    o_ref[...] = _dot(s, w_ref[...]) + b_ref[...]


def _ada(c_all, w_ada, b_ada):
    nb, d = c_all.shape
    n = w_ada.shape[1]
    tn = 1536
    return pl.pallas_call(
        _ada_kernel,
        grid=(n // tn,),
        in_specs=[pl.BlockSpec((nb, d), lambda j: (0, 0)),
                  pl.BlockSpec((d, tn), lambda j: (0, j)),
                  pl.BlockSpec((1, tn), lambda j: (0, j))],
        out_specs=pl.BlockSpec((nb, tn), lambda j: (0, j)),
        out_shape=jax.ShapeDtypeStruct((nb, n), F32),
        compiler_params=_params(("arbitrary",)),
    )(c_all, w_ada, b_ada.reshape(1, n))


def _inproj_kernel(x_ref, mod_ref, w_ref, om_ref, or_ref, u_ref):
    n = pl.program_id(2)
    gb, ts, d = x_ref.shape

    @pl.when(n == 0)
    def _():
        u = x_ref[...] * (1.0 + mod_ref[:, 1:2, :]) + mod_ref[:, 0:1, :]
        u_ref[...] = u.reshape(gb * ts, d).astype(BF16)

    res = jnp.dot(u_ref[...], w_ref[...], preferred_element_type=F32)

    @pl.when(n == 0)
    def _():
        om_ref[...] = res.reshape(om_ref.shape)

    @pl.when(n > 0)
    def _():
        or_ref[...] = res.reshape(or_ref.shape)


def _inproj(x, mod, w_packed, gb, ts):
    b, s, d = x.shape
    tn = MLA_SLOT
    nt = w_packed.shape[1] // tn
    return pl.pallas_call(
        _inproj_kernel,
        grid=(b // gb, s // ts, nt),
        in_specs=[pl.BlockSpec((gb, ts, d), lambda i, j, n: (i, j, 0)),
                  pl.BlockSpec((gb, 6, d), lambda i, j, n: (i, 0, 0)),
                  pl.BlockSpec((d, tn), lambda i, j, n: (0, n))],
        out_specs=[pl.BlockSpec((gb, ts, tn), lambda i, j, n: (i, j, 0)),
                   pl.BlockSpec((gb, ts, tn), lambda i, j, n: (i, j, jnp.maximum(n - 1, 0)))],
        out_shape=[jax.ShapeDtypeStruct((b, s, MLA_SLOT), F32),
                   jax.ShapeDtypeStruct((b, s, RW_SLOT), F32)],
        scratch_shapes=[pltpu.VMEM((gb * ts, d), BF16)],
        compiler_params=_params(("arbitrary", "arbitrary", "arbitrary")),
    )(x, mod, w_packed)


def _rms(x, g):
    return x * lax.rsqrt(jnp.mean(jnp.square(x), axis=-1, keepdims=True) + RMS_EPS) * g


def _rope_slot(slot, cos4, sin4):
    return slot * cos4 + pltpu.roll(slot, MLA_ROPE // 2, axis=1) * sin4


def _attn_kernel(*refs, tq, n_past, tk_past):
    if n_past:
        (pm_ref, cos_ref, sin_ref, gq_ref, gkv_ref, wuq_ref, wuk_ref, wuv_ref, ckv_ref, ckr_ref,
         mla_ref, kv_ref, kr_ref, qlat_ref, qrope_ref, m_ref, l_ref, acc_ref) = refs
    else:
        (pm_ref, cos_ref, sin_ref, gq_ref, gkv_ref, wuq_ref, wuk_ref, wuv_ref,
         mla_ref, kv_ref, kr_ref, qlat_ref, qrope_ref, m_ref, l_ref, acc_ref, kvs_ref, krs_ref) = refs
    j = pl.program_id(1)
    rows = MLA_HEADS * tq
    p = pm_ref[0]
    cos4 = cos_ref[...]
    sin4 = sin_ref[...]

    kv = _rms(p[:, Q_LORA:Q_LORA + KV_LORA], gkv_ref[...])
    kr = _rope_slot(p[:, Q_LORA + KV_LORA:MLA_SLOT], cos4, sin4)[:, :MLA_ROPE]
    kv_ref[0] = kv
    kr_ref[0] = kr
    kv_b = kv.astype(BF16)
    kr_b = kr.astype(BF16)

    q = _dot(_rms(p[:, :Q_LORA], gq_ref[...]), wuq_ref[...])
    for h in range(MLA_HEADS):
        qlat = _dot(q[:, h * MLA_NOPE:(h + 1) * MLA_NOPE], wuk_ref[h])
        rot = _rope_slot(q[:, MLA_WIDTH + h * LANES:MLA_WIDTH + (h + 1) * LANES], cos4, sin4)
        qlat_ref[h * tq:(h + 1) * tq, :] = (qlat * SOFTMAX_SCALE).astype(BF16)
        qrope_ref[h * tq:(h + 1) * tq, :] = (rot[:, :MLA_ROPE] * SOFTMAX_SCALE).astype(BF16)

    m_ref[...] = jnp.full(m_ref.shape, -jnp.inf, F32)
    l_ref[...] = jnp.zeros(l_ref.shape, F32)
    acc_ref[...] = jnp.zeros(acc_ref.shape, F32)

    grp_rows = max(tq, ATT_TQ)
    n_grp = rows // grp_rows

    def flash_step(kvb, krb, mask, kr_transposed=False):
        def scores(h):
            rs = slice(h * grp_rows, (h + 1) * grp_rows)
            rope = _dot(qrope_ref[rs, :], krb) if kr_transposed else _dot_nt(qrope_ref[rs, :], krb)
            return _dot_nt(qlat_ref[rs, :], kvb) + rope

        pending = None
        s_next = scores(0)
        for h in range(n_grp):
            rs = slice(h * grp_rows, (h + 1) * grp_rows)
            s = s_next
            if h + 1 < n_grp:
                s_next = scores(h + 1)
            if mask is not None:
                s = jnp.where(mask, s, -jnp.inf)
            m_prev = m_ref[rs, :]
            m_new = jnp.maximum(m_prev, jnp.max(s, axis=-1, keepdims=True))
            alpha = jnp.exp(m_prev - m_new)
            pr = jnp.exp(s - m_new)
            l_ref[rs, :] = alpha * l_ref[rs, :] + jnp.sum(pr, axis=-1, keepdims=True)
            m_ref[rs, :] = m_new
            pv = _dot(pr, kvb)
            if pending is not None:
                prs, palpha, ppv = pending
                acc_ref[prs, :] = palpha * acc_ref[prs, :] + ppv
            pending = (rs, alpha, pv)
        prs, palpha, ppv = pending
        acc_ref[prs, :] = palpha * acc_ref[prs, :] + ppv

    if n_past:
        def past_body(kb, carry):
            off = pl.multiple_of(kb * tk_past, tk_past)
            flash_step(ckv_ref[0, pl.ds(off, tk_past), :].astype(BF16),
                       ckr_ref[0, :, pl.ds(off, tk_past)].astype(BF16), None, kr_transposed=True)
            return carry
        lax.fori_loop(0, n_past // tk_past, past_body, 0)
        flash_step(kv_b, kr_b, None)
    else:
        off_j = pl.multiple_of(j * tq, tq)
        kvs_ref[pl.ds(off_j, tq), :] = kv_b
        krs_ref[pl.ds(off_j, tq), :] = kr_b

        def prev_body(kb, carry):
            off = pl.multiple_of(kb * 2 * tq, 2 * tq)
            flash_step(kvs_ref[pl.ds(off, 2 * tq), :], krs_ref[pl.ds(off, 2 * tq), :], None)
            return carry
        lax.fori_loop(0, j // 2, prev_body, 0)

        @pl.when(j % 2 == 0)
        def _():
            q_chunk = (_iota((grp_rows, tq), 0) & (tq - 1)) >> 6
            k_chunk = _iota((grp_rows, tq), 1) >> 6
            flash_step(kv_b, kr_b, k_chunk <= q_chunk)

        @pl.when(j % 2 == 1)
        def _():
            off = pl.multiple_of((j - 1) * tq, tq)
            q_chunk = ((_iota((grp_rows, 2 * tq), 0) & (tq - 1)) >> 6) + tq // CHUNK
            k_chunk = _iota((grp_rows, 2 * tq), 1) >> 6
            flash_step(kvs_ref[pl.ds(off, 2 * tq), :], krs_ref[pl.ds(off, 2 * tq), :], k_chunk <= q_chunk)

    o = acc_ref[...] / l_ref[...]
    for h in range(MLA_HEADS):
        mla_ref[0, :, h * MLA_VDIM:(h + 1) * MLA_VDIM] = _dot(o[h * tq:(h + 1) * tq, :], wuv_ref[h]).astype(BF16)


def _attn(pm, cos4, sin4, gq, gkv, wuq_p, wuk_t, wuv_t, cache_kv, cache_kr, tq):
    b, s, _ = pm.shape
    n_past = 0 if cache_kv is None else cache_kv.shape[1]
    rows = MLA_HEADS * tq
    const2 = lambda i, j: (0, 0)
    const3 = lambda i, j: (0, 0, 0)
    in_specs = [pl.BlockSpec((1, tq, MLA_SLOT), lambda i, j: (i, j, 0)),
                pl.BlockSpec((tq, LANES), lambda i, j: (j, 0)),
                pl.BlockSpec((tq, LANES), lambda i, j: (j, 0)),
                pl.BlockSpec((1, Q_LORA), const2),
                pl.BlockSpec((1, KV_LORA), const2),
                pl.BlockSpec(wuq_p.shape, const2),
                pl.BlockSpec(wuk_t.shape, const3),
                pl.BlockSpec(wuv_t.shape, const3)]
    args = [pm, cos4, sin4, gq, gkv, wuq_p, wuk_t, wuv_t]
    scratch = [pltpu.VMEM((rows, KV_LORA), BF16), pltpu.VMEM((rows, MLA_ROPE), BF16),
               pltpu.VMEM((rows, 1), F32), pltpu.VMEM((rows, 1), F32), pltpu.VMEM((rows, KV_LORA), F32)]
    if n_past:
        in_specs += [pl.BlockSpec((1, n_past, KV_LORA), lambda i, j: (i, 0, 0)),
                     pl.BlockSpec((1, MLA_ROPE, n_past), lambda i, j: (i, 0, 0))]
        args += [cache_kv, cache_kr]
    else:
        scratch += [pltpu.VMEM((s, KV_LORA), BF16), pltpu.VMEM((s, MLA_ROPE), BF16)]
    return pl.pallas_call(
        functools.partial(_attn_kernel, tq=tq, n_past=n_past, tk_past=min(ATT_TK_PAST, max(n_past, 1))),
        grid=(b, s // tq),
        in_specs=in_specs,
        out_specs=[pl.BlockSpec((1, tq, MLA_WIDTH), lambda i, j: (i, j, 0)),
                   pl.BlockSpec((1, tq, KV_LORA), lambda i, j: (i, j, 0)),
                   pl.BlockSpec((1, tq, MLA_ROPE), lambda i, j: (i, j, 0))],
        out_shape=[jax.ShapeDtypeStruct((b, s, MLA_WIDTH), BF16),
                   jax.ShapeDtypeStruct((b, s, KV_LORA), F32),
                   jax.ShapeDtypeStruct((b, s, MLA_ROPE), F32)],
        scratch_shapes=scratch,
        compiler_params=_params(("arbitrary", "arbitrary")),
    )(*args)


def _rwkv_kernel(pr_ref, shift_ref, zt0_ref, mu_ref, vec_ref, w2_ref, a2_ref, g2_ref,
                 out_ref, ztn_ref,
                 z_ref, carry_ref, y_ref, kt_ref, rt_ref, kh_ref, bh_ref, kb_ref, bb_ref, v_ref, wc_ref, *, tt):
    j = pl.program_id(1)
    nj = pl.num_programs(1)
    c = RWKV_WIDTH
    n_pairs = RWKV_HEADS // RWKV_HPG
    n_chunks = tt // CHUNK

    bd_mask = (_iota((PAIR, PAIR), 0) >> 6) == (_iota((PAIR, PAIR), 1) >> 6)
    bd_grp = (_iota((GRP, GRP), 0) >> 6) == (_iota((GRP, GRP), 1) >> 6)
    hsl = lambda hl: slice(hl * RWKV_HEAD, (hl + 1) * RWKV_HEAD)

    @pl.when(j == 0)
    def _():
        carry_ref[...] = jnp.broadcast_to(shift_ref[0], carry_ref.shape)
        for pr in range(n_pairs):
            z_ref[pr] = jnp.zeros((GRP, GRP), F32)
            for hl in range(RWKV_HPG):
                z_ref[pr, hsl(hl), hsl(hl)] = zt0_ref[0, RWKV_HPG * pr + hl]

    rw = pr_ref[0]
    row = _iota((tt, 1), 0)
    prev = jnp.where(row == 0, carry_ref[0:1, :], pltpu.roll(rw, 1, axis=0))
    carry_ref[...] = jnp.broadcast_to(rw[tt - 1:tt, :], carry_ref.shape)
    xm = rw + (prev - rw) * mu_ref[...]

    w0 = vec_ref[0:1, :]
    a0 = vec_ref[1:2, :]
    k_k = vec_ref[2:3, :]
    k_a = vec_ref[3:4, :]
    r_k = vec_ref[4:5, :]
    lnx_g = vec_ref[5:6, :]
    lnx_b = vec_ref[6:7, :]

    r = xm[:, 0:c]
    k = xm[:, c:2 * c]
    v = xm[:, 2 * c:3 * c]
    lora = xm[:, 3 * c:3 * c + RW_LORA_SLOT]
    gd = xm[:, 3 * c + RW_LORA_SLOT:RW_SLOT]
    lw = -DECAY_SCALE * jax.nn.sigmoid(w0 + _dot(jnp.tanh(lora), w2_ref[...]))
    a = jax.nn.sigmoid(a0 + _dot(lora, a2_ref[...]))
    g = _dot(jax.nn.sigmoid(gd), g2_ref[...])

    ones_bd = jnp.where(bd_mask, 1.0, 0.0).astype(BF16)

    def head_sum(x):
        h, m, _ = _split3(x)
        cols = []
        for blk in range(c // LANES):
            sl = slice(blk * LANES, (blk + 1) * LANES)
            cols.append(jnp.dot(h[:, sl], ones_bd, preferred_element_type=F32)
                        + jnp.dot(m[:, sl], ones_bd, preferred_element_type=F32))
        return jnp.concatenate(cols, axis=1)

    kk = k * k_k
    kk = kk / jnp.maximum(jnp.sqrt(head_sum(kk * kk)), 1e-12)
    k_mod = k * (1.0 + (a - 1.0) * k_a)
    bonus = head_sum(r * k_mod * r_k) * v
    b_vec = kk * a

    tri = ((_iota((tt, tt), 0) >> 6) == (_iota((tt, tt), 1) >> 6)) & (_iota((tt, tt), 1) <= _iota((tt, tt), 0))
    tri_b = jnp.where(tri, 1.0, 0.0).astype(BF16)
    lw_h, lw_m, lw_l = _split3(lw)
    cum = (jnp.dot(tri_b, lw_h, preferred_element_type=F32) + jnp.dot(tri_b, lw_m, preferred_element_type=F32)
           + jnp.dot(tri_b, lw_l, preferred_element_type=F32))
    cum_end = jnp.concatenate(
        [jnp.broadcast_to(cum[(ci + 1) * CHUNK - 1:(ci + 1) * CHUNK, :], (CHUNK, c)) for ci in range(n_chunks)],
        axis=0)
    for ci in range(n_chunks):
        wc_ref[ci * SUBLANES:(ci + 1) * SUBLANES, :] = jnp.broadcast_to(
            jnp.exp(cum[(ci + 1) * CHUNK - 1:(ci + 1) * CHUNK, :]), (SUBLANES, c))
    e_inv = jnp.exp(-cum)
    e_end = jnp.exp(cum_end - cum)
    kt_ref[...] = (kk * jnp.exp(cum - lw)).astype(BF16)
    rt_ref[...] = (r * jnp.exp(cum)).astype(BF16)
    kh_ref[...] = (k_mod * e_inv).astype(BF16)
    bh_ref[...] = (b_vec * e_inv).astype(BF16)
    kb_ref[...] = (k_mod * e_end).astype(BF16)
    bb_ref[...] = (b_vec * e_end).astype(BF16)
    v_ref[...] = v.astype(BF16)

    t_i = _iota((CHUNK, GRP), 0)
    j_i = _iota((CHUNK, GRP), 1) & (RWKV_HEAD - 1)
    strict2 = j_i < t_i
    incl2 = j_i <= t_i
    blk16 = (j_i >> 4) == (t_i >> 4)
    eye2 = jnp.where(j_i == t_i, 1.0, 0.0)
    zero_b = jnp.zeros((GRP, GRP), BF16)

    def stack2(x):
        xb = x.astype(BF16)
        return jnp.where(bd_grp, jnp.concatenate([xb] * RWKV_HPG, axis=0), zero_b)

    def pmm(x, y):
        return jnp.dot(x.astype(BF16), stack2(y), preferred_element_type=F32)

    items = [(ci, pr) for ci in range(n_chunks) for pr in range(n_pairs)]
    n_it = len(items)
    rsl = lambda ci: slice(ci * CHUNK, (ci + 1) * CHUNK)
    csl = lambda pr: slice(pr * GRP, (pr + 1) * GRP)
    each = lambda fn: [fn(i) for i in range(n_it)]

    kt = [kt_ref[rsl(ci), csl(pr)] for ci, pr in items]
    rt = [rt_ref[rsl(ci), csl(pr)] for ci, pr in items]
    vv = [v_ref[rsl(ci), csl(pr)] for ci, pr in items]
    khbh = [jnp.concatenate([stack2(kh_ref[rsl(ci), csl(pr)]), stack2(bh_ref[rsl(ci), csl(pr)])], axis=0)
            for ci, pr in items]
    a4 = each(lambda i: _dot_nt(jnp.concatenate([kt[i], rt[i]], axis=0), khbh[i]))
    m_mat = each(lambda i: jnp.where(strict2, a4[i][:CHUNK, :GRP], 0.0))
    n_mat = each(lambda i: jnp.where(strict2, a4[i][:CHUNK, GRP:], 0.0))
    p_mat = each(lambda i: jnp.where(incl2, a4[i][CHUNK:, :GRP], 0.0))
    q_mat = each(lambda i: jnp.where(incl2, a4[i][CHUNK:, GRP:], 0.0))
    n_d = each(lambda i: jnp.where(blk16, n_mat[i], 0.0))
    n_o = each(lambda i: n_mat[i] - n_d[i])
    n2 = each(lambda i: pmm(n_d[i], n_d[i]))
    t1 = each(lambda i: pmm(eye2 - n_d[i], eye2 + n2[i]))
    n4 = each(lambda i: pmm(n2[i], n2[i]))
    t2 = each(lambda i: pmm(t1[i], eye2 + n4[i]))
    n8 = each(lambda i: pmm(n4[i], n4[i]))
    t_d = each(lambda i: pmm(t2[i], eye2 + n8[i]))
    x1 = each(lambda i: pmm(t_d[i], n_o[i]))
    x2 = each(lambda i: pmm(x1[i], x1[i]))
    t3 = each(lambda i: pmm(eye2 - x1[i], eye2 + x2[i]))
    t_mat = each(lambda i: pmm(t3[i], t_d[i]))
    mvpv = each(lambda i: jnp.dot(jnp.concatenate([m_mat[i], p_mat[i]], axis=0).astype(BF16), stack2(vv[i]),
                                  preferred_element_type=F32))
    tkmv = each(lambda i: jnp.dot(t_mat[i].astype(BF16),
                                  jnp.concatenate([stack2(kt[i]), stack2(mvpv[i][:CHUNK])], axis=1),
                                  preferred_element_type=F32))
    z = [z_ref[pr] for pr in range(n_pairs)]
    for ci in range(n_chunks):
        ids = [ci * n_pairs + pr for pr in range(n_pairs)]
        zr = [jnp.dot(jnp.concatenate([tkmv[i][:, :GRP].astype(BF16), rt[i]], axis=0), z[pr].astype(BF16),
                      preferred_element_type=F32) for pr, i in enumerate(ids)]
        sk = [zr[pr][:CHUNK] + tkmv[i][:, GRP:] for pr, i in enumerate(ids)]
        qs = [pmm(q_mat[i], sk[pr]) for pr, i in enumerate(ids)]
        upd = [_dot_tn(jnp.concatenate([kb_ref[rsl(ci), csl(pr)], bb_ref[rsl(ci), csl(pr)]], axis=0),
                       jnp.concatenate([vv[i], (-sk[pr]).astype(BF16)], axis=0)) for pr, i in enumerate(ids)]
        for pr, i in enumerate(ids):
            y_ref[rsl(ci), csl(pr)] = zr[pr][CHUNK:] + mvpv[i][CHUNK:] - qs[pr]
            wc_col = jnp.transpose(wc_ref[ci * SUBLANES:(ci + 1) * SUBLANES, csl(pr)])[:, 0:1]
            z[pr] = z[pr] * wc_col + jnp.where(bd_grp, upd[pr], 0.0)
    for pr in range(n_pairs):
        z_ref[pr] = z[pr]

    y = y_ref[...]
    mean = head_sum(y) * (1.0 / RWKV_HEAD)
    dlt = y - mean
    var = head_sum(dlt * dlt) * (1.0 / RWKV_HEAD)
    yn = dlt * lax.rsqrt(var + GN_EPS) * lnx_g + lnx_b
    out_ref[0] = ((yn + bonus) * g).astype(BF16)

    @pl.when(j == nj - 1)
    def _():
        for pr in range(n_pairs):
            for hl in range(RWKV_HPG):
                ztn_ref[0, RWKV_HPG * pr + hl] = z_ref[pr, hsl(hl), hsl(hl)]


def _rwkv(pr, shift_p, zt0, mu_p, vecs, w2_p, a2_p, g2_p, tt):
    b, s, _ = pr.shape
    c = RWKV_WIDTH
    const2 = lambda i, j: (0, 0)
    full = lambda shape: pl.BlockSpec(shape, const2)
    return pl.pallas_call(
        functools.partial(_rwkv_kernel, tt=tt),
        grid=(b, s // tt),
        in_specs=[pl.BlockSpec((1, tt, RW_SLOT), lambda i, j: (i, j, 0)),
                  pl.BlockSpec((1, 1, RW_SLOT), lambda i, j: (i, 0, 0)),
                  pl.BlockSpec((1, RWKV_HEADS, RWKV_HEAD, RWKV_HEAD), lambda i, j: (i, 0, 0, 0)),
                  full((1, RW_SLOT)), full((SUBLANES, c)),
                  full(w2_p.shape), full(a2_p.shape), full(g2_p.shape)],
        out_specs=[pl.BlockSpec((1, tt, c), lambda i, j: (i, j, 0)),
                   pl.BlockSpec((1, RWKV_HEADS, RWKV_HEAD, RWKV_HEAD), lambda i, j: (i, 0, 0, 0))],
        out_shape=[jax.ShapeDtypeStruct((b, s, c), BF16),
                   jax.ShapeDtypeStruct((b, RWKV_HEADS, RWKV_HEAD, RWKV_HEAD), F32)],
        scratch_shapes=[pltpu.VMEM((RWKV_HEADS // RWKV_HPG, GRP, GRP), F32),
                        pltpu.VMEM((SUBLANES, RW_SLOT), F32),
                        pltpu.VMEM((tt, c), F32)]
                       + [pltpu.VMEM((tt, c), BF16) for _ in range(7)]
                       + [pltpu.VMEM((SUBLANES * (tt // CHUNK), c), F32)],
        compiler_params=_params(("arbitrary", "arbitrary")),
    )(pr, shift_p, zt0, mu_p, vecs, w2_p, a2_p, g2_p)


def _layer_norm(h, g, b):
    mu = jnp.mean(h, axis=-1, keepdims=True)
    d = h - mu
    var = jnp.mean(d * d, axis=-1, keepdims=True)
    return d * lax.rsqrt(var + LN_EPS) * g + b


def _store_token_major(ref, x):
    n = x.shape[0]
    for c in range(D_MODEL // LANES):
        ref[pl.ds(c, n, stride=D_MODEL // LANES), :] = x[:, c * LANES:(c + 1) * LANES]


def _load_token_major(ref, n, lead=()):
    cols = [ref[lead + (pl.ds(c, n, stride=D_MODEL // LANES), slice(None))] for c in range(D_MODEL // LANES)]
    return jnp.concatenate(cols, axis=1)


def _outproj_kernel(*refs, n_own):
    if n_own is None:
        _outproj_tile(*refs)
        return
    tail_ref = refs[9]
    t = pl.program_id(0)

    @pl.when(t < n_own)
    def _():
        _outproj_tile(*refs[:9], *refs[10:])

    @pl.when(t >= n_own)
    def _():
        refs[11][...] = tail_ref[...]


def _outproj_tile(mla_ref, rwk_ref, x_ref, mod_ref, wo1_ref, wo2_ref, ln_ref, wr_ref, rb_ref,
                  x1_ref, u2_ref, route_ref, cnt_ref):
    gb, ts, d = x_ref.shape
    rows = gb * ts
    mix = (jnp.dot(mla_ref[...].reshape(rows, MLA_WIDTH), wo1_ref[...], preferred_element_type=F32)
           + jnp.dot(rwk_ref[...].reshape(rows, RWKV_WIDTH), wo2_ref[...], preferred_element_type=F32))
    h = DEEPNORM_ALPHA * x_ref[...] + mod_ref[:, 2:3, :] * mix.reshape(gb, ts, d)
    x1 = _layer_norm(h, ln_ref[0:1, :], ln_ref[1:2, :])
    u2 = (x1 * (1.0 + mod_ref[:, 4:5, :]) + mod_ref[:, 3:4, :]).reshape(rows, d)
    x1_ref[...] = x1.reshape(rows, d)
    _store_token_major(u2_ref, u2)

    uh, um, _ = _split3(u2)
    lg = (jnp.dot(uh, wr_ref[0], preferred_element_type=F32) + jnp.dot(um, wr_ref[0], preferred_element_type=F32)
          + jnp.dot(uh, wr_ref[1], preferred_element_type=F32)) + rb_ref[...]
    lane = _iota((rows, LANES), 1).astype(F32)
    neg = -jnp.inf
    big = float(LANES)
    gl = jnp.where(lane < N_GROUPS, lg, neg)
    gmax = jnp.max(gl, axis=-1, keepdims=True)
    p_grp = 1.0 / jnp.sum(jnp.exp(gl - gmax), axis=-1, keepdims=True)
    grp = jnp.min(jnp.where(gl == gmax, lane, big), axis=-1, keepdims=True)
    lo = N_GROUPS + grp * EXPERTS_PER_GROUP
    el = jnp.where((lane >= lo) & (lane < lo + EXPERTS_PER_GROUP), lg, neg)
    v1 = jnp.max(el, axis=-1, keepdims=True)
    i1 = jnp.min(jnp.where(el == v1, lane, big), axis=-1, keepdims=True)
    el2 = jnp.where(lane == i1, neg, el)
    v2 = jnp.max(el2, axis=-1, keepdims=True)
    i2 = jnp.min(jnp.where(el2 == v2, lane, big), axis=-1, keepdims=True)
    e2 = jnp.exp(v2 - v1)
    den = 1.0 / (1.0 + e2)
    route_ref[...] = jnp.where(lane == 0, i1 - N_GROUPS,
                               jnp.where(lane == 1, i2 - N_GROUPS,
                                         jnp.where(lane == 2, p_grp * den,
                                                   jnp.where(lane == 3, p_grp * e2 * den, 0.0))))
    hits = jnp.where((lane == i1 - N_GROUPS) | (lane == i2 - N_GROUPS), 1.0, 0.0)
    cnt_ref[0] = jnp.broadcast_to(jnp.sum(hits, axis=0, keepdims=True), (SUBLANES, LANES))


def _outproj(mla, rwk, x, mod, wo1, wo2, ln, wr3, rb, gb, ts, u2_tail=None):
    b, s, d = x.shape
    rows = gb * ts
    nj = s // ts
    n_own = (b // gb) * nj
    lg = d // LANES
    n_tail = 0 if u2_tail is None else u2_tail.shape[0] // (rows * lg)
    own = lambda t: jnp.minimum(t, n_own - 1)
    seq3 = lambda t: (own(t) // nj, own(t) % nj, 0)
    const2 = lambda t: (0, 0)
    once = pl.Buffered(1)
    in_specs = [pl.BlockSpec((gb, ts, MLA_WIDTH), seq3),
                pl.BlockSpec((gb, ts, RWKV_WIDTH), seq3),
                pl.BlockSpec((gb, ts, d), seq3),
                pl.BlockSpec((gb, 6, d), lambda t: (own(t) // nj, 0, 0)),
                pl.BlockSpec(wo1.shape, const2, pipeline_mode=once),
                pl.BlockSpec(wo2.shape, const2, pipeline_mode=once),
                pl.BlockSpec(ln.shape, const2),
                pl.BlockSpec(wr3.shape, lambda t: (0, 0, 0), pipeline_mode=once),
                pl.BlockSpec(rb.shape, const2)]
    args = [mla, rwk, x, mod, wo1, wo2, ln, wr3, rb]
    if n_tail:
        in_specs.append(pl.BlockSpec((rows * lg, LANES), lambda t: (jnp.maximum(t - n_own, 0), 0)))
        args.append(u2_tail)
    return pl.pallas_call(
        functools.partial(_outproj_kernel, n_own=n_own if n_tail else None),
        grid=(n_own + n_tail,),
        in_specs=in_specs,
        out_specs=[pl.BlockSpec((rows, d), lambda t: (own(t), 0)),
                   pl.BlockSpec((rows * lg, LANES), lambda t: (t, 0)),
                   pl.BlockSpec((rows, LANES), lambda t: (own(t), 0)),
                   pl.BlockSpec((1, SUBLANES, LANES), lambda t: (own(t), 0, 0))],
        out_shape=[jax.ShapeDtypeStruct((b * s, d), F32),
                   jax.ShapeDtypeStruct(((n_own + n_tail) * rows * lg, LANES), F32),
                   jax.ShapeDtypeStruct((b * s, LANES), F32),
                   jax.ShapeDtypeStruct((n_own, SUBLANES, LANES), F32)],
        compiler_params=_params(("arbitrary",)),
    )(*args)


def _moe_kernel(bexp_ref, code_ref, base_ref, nval_ref, nused_ref,
                u2_hbm, wg_ref, wu_ref, wd_ref,
                y_hbm,
                xbuf, ybuf, wgb, wub, wdb, gsem, ssem, *, n_tok):
    i = pl.program_id(0)
    nused = nused_ref[0]
    slot = i % 3

    lg = D_MODEL // LANES

    def gather_copy(src_tok, xs, r):
        return pltpu.make_async_copy(u2_hbm.at[pl.ds(pl.multiple_of(src_tok * lg, lg), lg)],
                                     xbuf.at[xs, pl.ds(r * lg, lg)], gsem.at[xs])

    def scatter_copy(blk, r, for_wait=False):
        if for_wait:
            dst = 0
        else:
            code = code_ref[base_ref[blk] + r]
            dst = ((code >> 1) & 1) * n_tok + (code >> 2)
        return pltpu.make_async_copy(ybuf.at[blk % 3, pl.ds(r * lg, lg)],
                                     y_hbm.at[pl.ds(pl.multiple_of(dst * lg, lg), lg)], ssem.at[blk % 3])

    def issue_gather(blk, xs):
        for r in range(MOE_BLOCK):
            gather_copy(code_ref[base_ref[blk] + r] >> 2, xs, r).start()

    def wait_gather(xs):
        def body(r, carry):
            gather_copy(0, xs, r).wait()
            return carry
        lax.fori_loop(0, MOE_BLOCK, body, 0, unroll=8)

    def scatter_partial(blk, start):
        def body(r, carry):
            if start:
                scatter_copy(blk, r).start()
            else:
                scatter_copy(blk, r, for_wait=True).wait()
            return carry
        lax.fori_loop(0, nval_ref[blk], body, 0)

    def wait_scatter(blk):
        @pl.when(nval_ref[blk] == MOE_BLOCK)
        def _():
            def body(r, carry):
                scatter_copy(blk, r, for_wait=True).wait()
                return carry
            lax.fori_loop(0, MOE_BLOCK, body, 0, unroll=8)

        @pl.when(nval_ref[blk] < MOE_BLOCK)
        def _():
            scatter_partial(blk, False)

    def compute(with_prev_scatter):
        issue_gather(jnp.minimum(i + 2, nused - 1), (i + 2) % 3)
        if with_prev_scatter:
            for r in range(MOE_BLOCK):
                scatter_copy(i - 1, r).start()
        xb = _load_token_major(xbuf, MOE_BLOCK, (slot,)).astype(BF16)
        hg = jnp.dot(xb, wgb[...], preferred_element_type=F32)
        hu = jnp.dot(xb, wub[...], preferred_element_type=F32)
        hh = (hg * jax.nn.sigmoid(hg) * hu).astype(BF16)
        _store_token_major(ybuf.at[i % 3], jnp.dot(hh, wdb[...], preferred_element_type=F32))

    @pl.when(i == 0)
    def _():
        issue_gather(0, 0)
        issue_gather(jnp.minimum(1, nused - 1), 1)

    @pl.when(i <= nused)
    def _():
        wait_gather(slot)

        @pl.when(i >= 3)
        def _():
            wait_scatter(i - 3)

    @pl.when(i < nused)
    def _():
        prev_e = bexp_ref[jnp.maximum(i - 1, 0)]

        @pl.when((i == 0) | (bexp_ref[i] != prev_e))
        def _():
            wgb[...] = wg_ref[0].astype(BF16)
            wub[...] = wu_ref[0].astype(BF16)
            wdb[...] = wd_ref[0].astype(BF16)

        prev_full = (i >= 1) & (nval_ref[jnp.maximum(i - 1, 0)] == MOE_BLOCK)

        @pl.when(prev_full)
        def _():
            compute(True)

        @pl.when(jnp.logical_not(prev_full))
        def _():
            compute(False)

            @pl.when(i >= 1)
            def _():
                scatter_partial(i - 1, True)

    @pl.when(i == nused)
    def _():
        wait_gather((i + 1) % 3)
        scatter_partial(i - 1, True)

        @pl.when(i >= 2)
        def _():
            wait_scatter(i - 2)
        wait_scatter(i - 1)


def _moe(u2_all, code, row_base, n_valid, block_expert, n_used, wg, wu, wd):
    d = wg.shape[1]
    lg = d // LANES
    n_tok = u2_all.shape[0] // lg
    n_blocks = block_expert.shape[0]
    de = wg.shape[2]

    def wmap(i, bexp, code_r, base, nval, nused):
        return (bexp[jnp.minimum(i, nused[0] - 1)], 0, 0)

    grid_spec = pltpu.PrefetchScalarGridSpec(
        num_scalar_prefetch=5,
        grid=(n_blocks + 1,),
        in_specs=[pl.BlockSpec(memory_space=pl.ANY),
                  pl.BlockSpec((1, d, de), wmap),
                  pl.BlockSpec((1, d, de), wmap),
                  pl.BlockSpec((1, de, d), wmap)],
        out_specs=pl.BlockSpec(memory_space=pl.ANY),
        scratch_shapes=[pltpu.VMEM((3, MOE_BLOCK * lg, LANES), F32), pltpu.VMEM((3, MOE_BLOCK * lg, LANES), F32),
                        pltpu.VMEM((d, de), BF16), pltpu.VMEM((d, de), BF16), pltpu.VMEM((de, d), BF16),
                        pltpu.SemaphoreType.DMA((3,)), pltpu.SemaphoreType.DMA((3,))])
    return pl.pallas_call(
        functools.partial(_moe_kernel, n_tok=n_tok),
        grid_spec=grid_spec,
        out_shape=jax.ShapeDtypeStruct((TOP_K * n_tok * lg, LANES), F32),
        compiler_params=_params(("arbitrary",)),
    )(block_expert, code, row_base, n_valid, n_used, u2_all, wg, wu, wd)


def _rank_kernel(route_ref, base_ref, o_ref):
    rows = route_ref.shape[0]
    lane = _iota((rows, LANES), 1).astype(F32)
    hit1 = lane == route_ref[:, 0:1]
    hit2 = lane == route_ref[:, 1:2]
    hits = jnp.where(hit1 | hit2, 1.0, 0.0).astype(BF16)
    earlier = jnp.where(_iota((rows, rows), 1) < _iota((rows, rows), 0), 1.0, 0.0).astype(BF16)
    pos = jnp.dot(earlier, hits, preferred_element_type=F32) + base_ref[0, 0:1, :]
    d1 = jnp.sum(jnp.where(hit1, pos, 0.0), axis=-1, keepdims=True)
    d2 = jnp.sum(jnp.where(hit2, pos, 0.0), axis=-1, keepdims=True)
    o_ref[...] = jnp.where(lane == 0, d1, jnp.where(lane == 1, d2, 0.0))


def _route_tables(route, tile_counts):
    n_tok = route.shape[0]
    n_asg = n_tok * TOP_K
    n_tiles = tile_counts.shape[0]
    rows = n_tok // n_tiles
    tc = tile_counts[:, 0, :].astype(jnp.int32)
    counts = jnp.sum(tc, axis=0)
    padded = (counts + MOE_BLOCK - 1) // MOE_BLOCK * MOE_BLOCK
    pad_ends = jnp.cumsum(padded)
    pad_starts = pad_ends - padded
    base = (pad_starts[None, :] + jnp.cumsum(tc, axis=0) - tc).astype(F32)
    base = jnp.broadcast_to(base[:, None, :], (n_tiles, SUBLANES, LANES))
    dest = pl.pallas_call(
        _rank_kernel,
        grid=(n_tiles,),
        in_specs=[pl.BlockSpec((rows, LANES), lambda i: (i, 0)),
                  pl.BlockSpec((1, SUBLANES, LANES), lambda i: (i, 0, 0))],
        out_specs=pl.BlockSpec((rows, LANES), lambda i: (i, 0)),
        out_shape=jax.ShapeDtypeStruct((n_tok, LANES), F32),
        compiler_params=_params(("arbitrary",)),
    )(route, base)
    dest = dest[:, 0:TOP_K].astype(jnp.int32).reshape(n_asg)
    n_blocks = -(-(n_asg + N_EXPERTS * (MOE_BLOCK - 1)) // MOE_BLOCK)
    _, asg_sorted = lax.sort((dest, jnp.arange(n_asg, dtype=jnp.int32)), num_keys=1)
    code = jnp.concatenate([asg_sorted * 2 + 1, jnp.zeros((MOE_BLOCK,), jnp.int32)])
    blk_start = jnp.arange(n_blocks, dtype=jnp.int32) * MOE_BLOCK
    block_expert = jnp.minimum(
        jnp.sum((pad_ends[None, :N_EXPERTS] <= blk_start[:, None]).astype(jnp.int32), axis=1),
        N_EXPERTS - 1).astype(jnp.int32)
    n_used = (pad_ends[N_EXPERTS - 1:N_EXPERTS] // MOE_BLOCK).astype(jnp.int32)
    filled = (pad_starts + counts)[:N_EXPERTS]
    n_valid = jnp.where(blk_start < pad_ends[N_EXPERTS - 1],
                        jnp.clip(filled[block_expert] - blk_start, 0, MOE_BLOCK), 0).astype(jnp.int32)
    pad_before = (pad_starts - (jnp.cumsum(counts) - counts))[:N_EXPERTS]
    row_base = jnp.clip(blk_start - pad_before[block_expert], 0, n_asg).astype(jnp.int32)
    return code, row_base, n_valid, block_expert, n_used


def _final_kernel(x1_ref, y0_ref, y1_ref, route_ref, mod_ref, ln_ref, o_ref):
    gb, ts, d = o_ref.shape
    y0 = _load_token_major(y0_ref, gb * ts)
    y1 = _load_token_major(y1_ref, gb * ts)
    moe = (route_ref[:, 2:3] * y0 + route_ref[:, 3:4] * y1).reshape(gb, ts, d)
    h = DEEPNORM_ALPHA * x1_ref[...].reshape(gb, ts, d) + mod_ref[:, 5:6, :] * moe
    o_ref[...] = _layer_norm(h, ln_ref[0:1, :], ln_ref[1:2, :])


def _final(x1, y, route, mod, ln, b, s, gb, ts, tile0, n_tok):
    d = x1.shape[1]
    rows = gb * ts
    nj = s // ts
    k_off = n_tok // rows
    return pl.pallas_call(
        _final_kernel,
        grid=(b // gb, nj),
        in_specs=[pl.BlockSpec((rows, d), lambda i, j: (i * nj + j, 0)),
                  pl.BlockSpec((rows * (d // LANES), LANES), lambda i, j: (tile0 + i * nj + j, 0)),
                  pl.BlockSpec((rows * (d // LANES), LANES), lambda i, j: (k_off + tile0 + i * nj + j, 0)),
                  pl.BlockSpec((rows, LANES), lambda i, j: (i * nj + j, 0)),
                  pl.BlockSpec((gb, 6, d), lambda i, j: (i, 0, 0)),
                  pl.BlockSpec(ln.shape, lambda i, j: (0, 0))],
        out_specs=pl.BlockSpec((gb, ts, d), lambda i, j: (i, j, 0)),
        out_shape=jax.ShapeDtypeStruct((b, s, d), F32),
        compiler_params=_params(("arbitrary", "arbitrary")),
    )(x1, y, y, route, mod, ln)


def _rope_tables(n_past, s):
    inv_freq = ROPE_THETA ** (-jnp.arange(0, MLA_ROPE, 2, dtype=F32) / MLA_ROPE)
    ang = jnp.arange(n_past, n_past + s).astype(F32)[:, None] * inv_freq[None, :]
    cos, sin = jnp.cos(ang), jnp.sin(ang)
    return jnp.concatenate([cos, cos, cos, cos], axis=1), jnp.concatenate([-sin, sin, -sin, sin], axis=1)


def _pad_cols(w, n):
    return jnp.pad(w, ((0, 0), (0, n - w.shape[1])))


def kernel(x_prompt, x_sample, c_prompt, c_sample, cache_kv_latent, cache_k_rope, state_shift, state_wkv, w_ada, b_ada, w_in, q_norm_g, w_uq, kv_norm_g, w_uk, w_uv, rwkv_mu, rwkv_w0, rwkv_w2, rwkv_a0, rwkv_a2, rwkv_g2, rwkv_k_k, rwkv_k_a, rwkv_r_k, rwkv_lnx_g, rwkv_lnx_b, w_out, ln1_g, ln1_b, router_group_w, router_group_b, router_expert_w, router_expert_b, expert_w_gate, expert_w_up, expert_w_down, ln2_g, ln2_b):
    depth = w_in.shape[0]
    assert depth == 1
    bp, sp, d = x_prompt.shape
    bs, ss, _ = x_sample.shape
    n_past = cache_kv_latent.shape[2]
    assert d == D_MODEL and sp % ROW_TILE == 0 and ss == CHUNK and bs % (ROW_TILE // CHUNK) == 0
    assert sp % ATT_TQ == 0 and n_past % ATT_TK_PAST == 0
    gb_s = ROW_TILE // ss
    mla_proj = Q_LORA + KV_LORA + MLA_ROPE

    wi = w_in[0]
    w_in_p = jnp.concatenate(
        [wi[:, :mla_proj], wi[:, mla_proj - MLA_ROPE:mla_proj], _pad_cols(wi[:, mla_proj:], RW_SLOT)],
        axis=1).astype(BF16)
    wq = w_uq[0].reshape(Q_LORA, MLA_HEADS, MLA_QK)
    wq_rope = wq[:, :, MLA_NOPE:]
    wuq_p = jnp.concatenate(
        [wq[:, :, :MLA_NOPE].reshape(Q_LORA, MLA_WIDTH),
         jnp.concatenate([wq_rope, wq_rope], axis=2).reshape(Q_LORA, MLA_HEADS * LANES)], axis=1).astype(BF16)
    wuk_t = jnp.transpose(w_uk[0], (1, 2, 0)).astype(BF16)
    wuv_t = jnp.transpose(w_uv[0], (1, 0, 2)).astype(BF16)
    gq = q_norm_g[0].reshape(1, Q_LORA)
    gkv = kv_norm_g[0].reshape(1, KV_LORA)
    mu_p = _pad_cols(rwkv_mu[0].reshape(1, RWKV_PROJ), RW_SLOT)
    zeros_c = jnp.zeros((RWKV_WIDTH,), F32)
    vecs = jnp.stack([rwkv_w0[0], rwkv_a0[0], rwkv_k_k[0], rwkv_k_a[0], rwkv_r_k[0].reshape(RWKV_WIDTH),
                      rwkv_lnx_g[0], rwkv_lnx_b[0], zeros_c])
    w2_p = jnp.concatenate([rwkv_w2[0], jnp.zeros((AAA_LORA, RWKV_WIDTH), F32)], axis=0).astype(BF16)
    a2_p = jnp.concatenate([jnp.zeros((DECAY_LORA, RWKV_WIDTH), F32), rwkv_a2[0]], axis=0).astype(BF16)
    g2_p = jnp.concatenate([rwkv_g2[0], jnp.zeros((RW_GATE_SLOT - GATE_LORA, RWKV_WIDTH), F32)],
                           axis=0).astype(BF16)
    wo1 = w_out[0][:MLA_WIDTH].astype(BF16)
    wo2 = w_out[0][MLA_WIDTH:].astype(BF16)
    ln1 = jnp.stack([ln1_g[0], ln1_b[0]])
    ln2 = jnp.stack([ln2_g[0], ln2_b[0]])
    wr = _pad_cols(jnp.concatenate([router_group_w[0], router_expert_w[0]], axis=1), LANES)
    wr3 = jnp.stack(_split3(wr))
    rb = _pad_cols(jnp.concatenate([router_group_b[0], router_expert_b[0]]).reshape(1, -1), LANES)

    mod = _ada(jnp.concatenate([c_prompt, c_sample], axis=0), w_ada[0], b_ada[0]).reshape(bp + bs, 6, d)
    mod_p, mod_s = mod[:bp], mod[bp:]

    def mix_group(x, mod_g, gb, ts, tq, tt, cache_kv, cache_kr, shift_prev, wkv_prev, u2_tail=None):
        b, s, _ = x.shape
        pm, prw = _inproj(x, mod_g, w_in_p, gb, ts)
        n_p = 0 if cache_kv is None else cache_kv.shape[1]
        cos4, sin4 = _rope_tables(n_p, s)
        mla, kv_new, kr_new = _attn(pm, cos4, sin4, gq, gkv, wuq_p, wuk_t, wuv_t, cache_kv, cache_kr, tq)
        shift_p = _pad_cols(shift_prev.reshape(b, RWKV_PROJ), RW_SLOT).reshape(b, 1, RW_SLOT)
        zt0 = jnp.swapaxes(wkv_prev.astype(F32), -1, -2)
        rwk, ztn = _rwkv(prw, shift_p, zt0, mu_p, vecs, w2_p, a2_p, g2_p, tt)
        x1, u2, route, cnt = _outproj(mla, rwk, x, mod_g, wo1, wo2, ln1, wr3, rb, gb, ts, u2_tail)
        shift_new = prw[:, s - 1:s, :RWKV_PROJ]
        return x1, u2, (route, cnt), kv_new, kr_new, shift_new, jnp.swapaxes(ztn, -1, -2)

    zero_shift = jnp.zeros((bp, 1, RWKV_PROJ), F32)
    zero_wkv = jnp.zeros((bp, RWKV_HEADS, RWKV_HEAD, RWKV_HEAD), F32)
    x1_s, u2_s, route_s, kv_s, kr_s, sh_s, wkv_s = mix_group(
        x_sample, mod_s, gb_s, ss, ss, ss, cache_kv_latent[0], jnp.swapaxes(cache_k_rope[0], 1, 2),
        state_shift[0], state_wkv[0])
    x1_p, u2_all, route_p, kv_p, kr_p, sh_p, wkv_p = mix_group(
        x_prompt, mod_p, 1, ROW_TILE, ATT_TQ, RWKV_TT, None, None, zero_shift, zero_wkv, u2_s)

    n_p_tok = bp * sp
    n_tok = n_p_tok + bs * ss
    code, row_base, n_valid, block_expert, n_used = _route_tables(
        jnp.concatenate([route_p[0], route_s[0]], axis=0), jnp.concatenate([route_p[1], route_s[1]], axis=0))
    y = _moe(u2_all, code, row_base, n_valid, block_expert, n_used,
             expert_w_gate[0], expert_w_up[0], expert_w_down[0])

    out_p = _final(x1_p, y, route_p[0], mod_p, ln2, bp, sp, 1, ROW_TILE, 0, n_tok)
    out_s = _final(x1_s, y, route_s[0], mod_s, ln2, bs, ss, gb_s, ss, n_p_tok // ROW_TILE, n_tok)
    return (out_p, out_s, kv_p[None], kr_p[None], sh_p[None], wkv_p[None],
            kv_s[None], kr_s[None], sh_s[None], wkv_s[None])
```
